```python
import jax, jax.numpy as jnp
from jax import lax
import numpy as np

D_MODEL = 2048
BATCH = 8
SEQ = 2048
DEPTH = 2
DEC_BATCH = 128
DEC_SEQ = 8
PAST_LEN = 2048
PAGE_SIZE = 128

N_AG_LAYERS = (DEPTH + 1) // 2
N_CV_LAYERS = DEPTH // 2
HEAD_DIM = 128
ATT_W = D_MODEL // 2
ATT_HEADS = ATT_W // HEAD_DIM
GM_CH = 128
GM_W = D_MODEL // 2
GM_GROUPS = GM_W // GM_CH
CHUNK = 128
Q_BLOCK = 128
AG_SPLITS = (ATT_W, 2 * ATT_W, 3 * ATT_W, 3 * ATT_W + ATT_HEADS, 3 * ATT_W + ATT_HEADS + GM_W)
IN_AG = 3 * ATT_W + ATT_HEADS + 2 * GM_W
CONV_W = 3
PEER_HEADS = 8
PEER_TOPK = 16
N_KEYS = 128
N_EXPERTS = N_KEYS * N_KEYS
D_KEY = 256
PEER_BLOCK = 128
N_MOD = 6
EPS = 1e-6

kernel_name = 'hybrid_fox_gmlp_shortconv_peer_step'


def rms_norm(x, g):
    x32 = x.astype(jnp.float32)
    y = x32 * lax.rsqrt(jnp.mean(x32 * x32, axis=-1, keepdims=True) + EPS)
    return (y * g.astype(jnp.float32)).astype(x.dtype)


def ada_modulation(c, w, b):
    m = jax.nn.silu(c) @ w + b
    return m.reshape(c.shape[0], N_MOD, D_MODEL)


def modulate(x, g, shift, scale):
    return rms_norm(x, g) * (1 + scale) + shift


def fox_logits(q, k, fq, fk, q_pos, k_pos):
    s = jnp.einsum('bqhd,bkhd->bhqk', q, k, preferred_element_type=jnp.float32) * (HEAD_DIM ** -0.5)
    s = s + jnp.transpose(fq, (0, 2, 1))[..., :, None] - jnp.transpose(fk, (0, 2, 1))[..., None, :]
    mask = k_pos[None, :] <= q_pos[:, None]
    return jnp.where(mask, s, -jnp.inf)


def fox_prompt(q, k, v, logf):
    b, t = q.shape[:2]
    f = jnp.cumsum(logf.astype(jnp.float32), axis=1)
    k_pos = jnp.arange(t)

    def block(i):
        s0 = i * Q_BLOCK
        qb = lax.dynamic_slice_in_dim(q, s0, Q_BLOCK, axis=1)
        fb = lax.dynamic_slice_in_dim(f, s0, Q_BLOCK, axis=1)
        q_pos = s0 + jnp.arange(Q_BLOCK)
        p = jax.nn.softmax(fox_logits(qb, k, fb, f, q_pos, k_pos), axis=-1)
        return jnp.einsum('bhqk,bkhd->bqhd', p.astype(v.dtype), v)

    o = lax.map(block, jnp.arange(t // Q_BLOCK))
    return jnp.moveaxis(o, 0, 1).reshape(b, t, ATT_HEADS, HEAD_DIM)


def fox_sample(q, k, v, logf, k_past, v_past, logf_past):
    n = q.shape[1]
    past = k_past.shape[1]
    f_past = jnp.cumsum(logf_past.astype(jnp.float32), axis=1)
    f_new = f_past[:, -1:, :] + jnp.cumsum(logf.astype(jnp.float32), axis=1)
    q_pos = past + jnp.arange(n)
    s = jnp.concatenate([fox_logits(q, k_past, f_new, f_past, q_pos, jnp.arange(past)),
                         fox_logits(q, k, f_new, f_new, q_pos, q_pos)], axis=-1)
    p = jax.nn.softmax(s, axis=-1).astype(v.dtype)
    return (jnp.einsum('bhqk,bkhd->bqhd', p[..., :past], v_past)
            + jnp.einsum('bhqk,bkhd->bqhd', p[..., past:], v))


def chunk_spatial_gate(u, v, w_s, b_s):
    b, t = v.shape[:2]
    nc = -(-t // CHUNK)
    pad = nc * CHUNK - t
    vp = jnp.pad(v, ((0, 0), (0, pad), (0, 0), (0, 0))).reshape(b, nc, CHUNK, GM_GROUPS, GM_CH)
    w = jnp.tril(w_s)
    s = jnp.einsum('gij,bcjgd->bcigd', w, vp) + jnp.transpose(b_s)[None, None, :, :, None]
    s = s.reshape(b, nc * CHUNK, GM_GROUPS, GM_CH)[:, :t]
    return u * s


def ag_inputs(h, w_in, b_f, q_g, k_g):
    b, t, _ = h.shape
    q, k, v, fl, u, gv = jnp.split(h @ w_in, AG_SPLITS, axis=-1)
    q = rms_norm(q.reshape(b, t, ATT_HEADS, HEAD_DIM), q_g)
    k = rms_norm(k.reshape(b, t, ATT_HEADS, HEAD_DIM), k_g)
    v = v.reshape(b, t, ATT_HEADS, HEAD_DIM)
    logf = jax.nn.log_sigmoid((fl + b_f).astype(jnp.float32))
    u = jax.nn.gelu(u).reshape(b, t, GM_GROUPS, GM_CH)
    gv = jax.nn.gelu(gv).reshape(b, t, GM_GROUPS, GM_CH)
    return q, k, v, logf, u, gv


def ag_output(attn, u, gv, w_s, b_s, w_out):
    b, t = u.shape[:2]
    gm = chunk_spatial_gate(u, gv, w_s, b_s)
    mixed = jnp.concatenate([attn.reshape(b, t, ATT_W), gm.reshape(b, t, GM_W)], axis=-1)
    return mixed @ w_out


def short_conv(z, buf, w, bias):
    t = z.shape[1]
    zp = jnp.concatenate([buf, z], axis=1)
    y = bias + w[0] * zp[:, 0:t]
    for j in range(1, CONV_W):
        y = y + w[j] * zp[:, j:j + t]
    return y, zp[:, t:]


def conv_mixer(h, buf, w_in, cw, cb, w_out):
    bg, cg, z = jnp.split(h @ w_in, 3, axis=-1)
    y, new_buf = short_conv(cg * z, buf, cw, cb)
    return (bg * y) @ w_out, new_buf


def peer_ffn(h, w_q, sub_keys, u_tab, v_tab):
    shp = h.shape
    x = h.reshape(-1, D_MODEL)
    n = x.shape[0]
    half = D_KEY // 2
    q = (x @ w_q).reshape(n, PEER_HEADS, D_KEY)
    s1 = jnp.einsum('nhd,kd->nhk', q[..., :half], sub_keys[0], preferred_element_type=jnp.float32)
    s2 = jnp.einsum('nhd,kd->nhk', q[..., half:], sub_keys[1], preferred_element_type=jnp.float32)
    t1, i1 = lax.top_k(s1, PEER_TOPK)
    t2, i2 = lax.top_k(s2, PEER_TOPK)
    cand = (t1[..., :, None] + t2[..., None, :]).reshape(n, PEER_HEADS, PEER_TOPK * PEER_TOPK)
    ts, ic = lax.top_k(cand, PEER_TOPK)
    e1 = jnp.take_along_axis(i1, ic // PEER_TOPK, axis=-1)
    e2 = jnp.take_along_axis(i2, ic % PEER_TOPK, axis=-1)
    experts = (e1 * N_KEYS + e2).reshape(n, PEER_HEADS * PEER_TOPK)
    gates = jax.nn.softmax(ts, axis=-1).reshape(n, PEER_HEADS * PEER_TOPK).astype(x.dtype)
    nb = -(-n // PEER_BLOCK)
    pad = nb * PEER_BLOCK - n
    xb = jnp.pad(x, ((0, pad), (0, 0))).reshape(nb, PEER_BLOCK, D_MODEL)
    eb = jnp.pad(experts, ((0, pad), (0, 0))).reshape(nb, PEER_BLOCK, PEER_HEADS * PEER_TOPK)
    gb = jnp.pad(gates, ((0, pad), (0, 0))).reshape(nb, PEER_BLOCK, PEER_HEADS * PEER_TOPK)

    def expert_block(args):
        xk, ek, gk = args
        a = jnp.einsum('nd,nkd->nk', xk, u_tab[ek])
        return jnp.einsum('nk,nkd->nd', jax.nn.gelu(a) * gk, v_tab[ek])

    y = lax.map(expert_block, (xb, eb, gb)).reshape(nb * PEER_BLOCK, D_MODEL)[:n]
    return y.reshape(shp)


def setup_inputs(seed: int = 0) -> dict:
    key = jax.random.key(seed)
    ks = jax.random.split(key, 32)
    n_pages = PAST_LEN // PAGE_SIZE
    n_used = DEC_BATCH * n_pages
    n_pool = n_used + max(1, n_used // 4)

    def nrm(k, shape, scale):
        return jax.random.normal(k, shape, jnp.float32) * scale

    page_table = jax.random.permutation(ks[8], n_pool)[:n_used].reshape(DEC_BATCH, n_pages).astype(jnp.int32)
    return {
        'x_prompt': nrm(ks[0], (BATCH, SEQ, D_MODEL), 1.0),
        'x_sample': nrm(ks[1], (DEC_BATCH, DEC_SEQ, D_MODEL), 1.0),
        'c_prompt': nrm(ks[2], (BATCH, D_MODEL), 1.0),
        'c_sample': nrm(ks[3], (DEC_BATCH, D_MODEL), 1.0),
        'cache_k': nrm(ks[4], (N_AG_LAYERS, n_pool, PAGE_SIZE, ATT_HEADS, HEAD_DIM), 1.0),
        'cache_v': nrm(ks[5], (N_AG_LAYERS, n_pool, PAGE_SIZE, ATT_HEADS, HEAD_DIM), 1.0),
        'cache_logf': jax.nn.log_sigmoid(2.0 + nrm(ks[6], (N_AG_LAYERS, n_pool, PAGE_SIZE, ATT_HEADS), 1.0)),
        'page_table': page_table,
        'state_conv': nrm(ks[7], (N_CV_LAYERS, DEC_BATCH, CONV_W - 1, D_MODEL), 1.0),
        'w_ada': nrm(ks[9], (DEPTH, D_MODEL, N_MOD * D_MODEL), 0.5 * D_MODEL ** -0.5),
        'b_ada': nrm(ks[10], (DEPTH, N_MOD * D_MODEL), 0.02),
        'norm_mix': 1.0 + nrm(ks[11], (DEPTH, D_MODEL), 0.05),
        'norm_ffn': 1.0 + nrm(ks[12], (DEPTH, D_MODEL), 0.05),
        'w_in_ag': nrm(ks[13], (N_AG_LAYERS, D_MODEL, IN_AG), D_MODEL ** -0.5),
        'b_fgate': 2.0 + nrm(ks[14], (N_AG_LAYERS, ATT_HEADS), 0.1),
        'q_norm': 1.0 + nrm(ks[15], (N_AG_LAYERS, HEAD_DIM), 0.05),
        'k_norm': 1.0 + nrm(ks[16], (N_AG_LAYERS, HEAD_DIM), 0.05),
        'w_spatial': nrm(ks[17], (N_AG_LAYERS, GM_GROUPS, CHUNK, CHUNK), CHUNK ** -0.5),
        'b_spatial': 1.0 + nrm(ks[18], (N_AG_LAYERS, GM_GROUPS, CHUNK), 0.1),
        'w_out_ag': nrm(ks[19], (N_AG_LAYERS, ATT_W + GM_W, D_MODEL), (ATT_W + GM_W) ** -0.5),
        'w_in_conv': nrm(ks[20], (N_CV_LAYERS, D_MODEL, 3 * D_MODEL), D_MODEL ** -0.5),
        'conv_w': nrm(ks[21], (N_CV_LAYERS, CONV_W, D_MODEL), CONV_W ** -0.5),
        'conv_b': nrm(ks[22], (N_CV_LAYERS, D_MODEL), 0.02),
        'w_out_conv': nrm(ks[23], (N_CV_LAYERS, D_MODEL, D_MODEL), D_MODEL ** -0.5),
        'peer_wq': nrm(ks[24], (DEPTH, D_MODEL, PEER_HEADS * D_KEY), D_MODEL ** -0.5),
        'peer_keys': nrm(ks[25], (DEPTH, 2, N_KEYS, D_KEY // 2), (D_KEY // 2) ** -0.5),
        'peer_u': nrm(ks[26], (DEPTH, N_EXPERTS, D_MODEL), D_MODEL ** -0.5),
        'peer_v': nrm(ks[27], (DEPTH, N_EXPERTS, D_MODEL), PEER_HEADS ** -0.5),
    }


def reference(x_prompt, x_sample, c_prompt, c_sample, cache_k, cache_v, cache_logf, page_table, state_conv,
              w_ada, b_ada, norm_mix, norm_ffn, w_in_ag, b_fgate, q_norm, k_norm, w_spatial, b_spatial,
              w_out_ag, w_in_conv, conv_w, conv_b, w_out_conv, peer_wq, peer_keys, peer_u, peer_v):
    xp, xs = x_prompt, x_sample
    bp, bs = xp.shape[0], xs.shape[0]
    kp_l, vp_l, fp_l = [], [], []
    ks_l, vs_l, fs_l, gs_l = [], [], [], []
    cp_l, cs_l = [], []
    for layer in range(DEPTH):
        mp = ada_modulation(c_prompt, w_ada[layer], b_ada[layer])
        ms = ada_modulation(c_sample, w_ada[layer], b_ada[layer])
        hp = modulate(xp, norm_mix[layer], mp[:, 0:1], mp[:, 1:2])
        hs = modulate(xs, norm_mix[layer], ms[:, 0:1], ms[:, 1:2])
        if layer % 2 == 0:
            a = layer // 2
            qp, kp, vp, lfp, up, gvp = ag_inputs(hp, w_in_ag[a], b_fgate[a], q_norm[a], k_norm[a])
            qs, kq, vq, lfs, us, gvs = ag_inputs(hs, w_in_ag[a], b_fgate[a], q_norm[a], k_norm[a])
            attn_p = fox_prompt(qp, kp, vp, lfp)
            k_past = cache_k[a, page_table].reshape(bs, -1, ATT_HEADS, HEAD_DIM)
            v_past = cache_v[a, page_table].reshape(bs, -1, ATT_HEADS, HEAD_DIM)
            lf_past = cache_logf[a, page_table].reshape(bs, -1, ATT_HEADS)
            attn_s = fox_sample(qs, kq, vq, lfs, k_past, v_past, lf_past)
            op = ag_output(attn_p, up, gvp, w_spatial[a], b_spatial[a], w_out_ag[a])
            osm = ag_output(attn_s, us, gvs, w_spatial[a], b_spatial[a], w_out_ag[a])
            kp_l.append(kp)
            vp_l.append(vp)
            fp_l.append(lfp)
            ks_l.append(kq)
            vs_l.append(vq)
            fs_l.append(lfs)
            gs_l.append(gvs.reshape(bs, xs.shape[1], GM_W))
        else:
            ci = layer // 2
            zero_buf = jnp.zeros((bp, CONV_W - 1, D_MODEL), xp.dtype)
            op, bufp = conv_mixer(hp, zero_buf, w_in_conv[ci], conv_w[ci], conv_b[ci], w_out_conv[ci])
            osm, bufs = conv_mixer(hs, state_conv[ci], w_in_conv[ci], conv_w[ci], conv_b[ci], w_out_conv[ci])
            cp_l.append(bufp)
            cs_l.append(bufs)
        xp = xp + mp[:, 2:3] * op
        xs = xs + ms[:, 2:3] * osm
        hp = modulate(xp, norm_ffn[layer], mp[:, 3:4], mp[:, 4:5])
        hs = modulate(xs, norm_ffn[layer], ms[:, 3:4], ms[:, 4:5])
        xp = xp + mp[:, 5:6] * peer_ffn(hp, peer_wq[layer], peer_keys[layer], peer_u[layer], peer_v[layer])
        xs = xs + ms[:, 5:6] * peer_ffn(hs, peer_wq[layer], peer_keys[layer], peer_u[layer], peer_v[layer])
    return (xp, xs, jnp.stack(kp_l), jnp.stack(vp_l), jnp.stack(fp_l), jnp.stack(ks_l), jnp.stack(vs_l),
            jnp.stack(fs_l), jnp.stack(gs_l), jnp.stack(cp_l), jnp.stack(cs_l))
```

```python
import functools

import jax
import jax.numpy as jnp
from jax import lax
from jax.experimental import pallas as pl
from jax.experimental.pallas import tpu as pltpu

F32 = jnp.float32
BF16 = jnp.bfloat16

EPS = 1e-6
LANES = 128
SUBLANES = 8
HEAD_DIM = 128
N_MOD = 6
PEER_HEADS = 8
PEER_TOPK = 16
N_KEYS = 128
PAGE_SIZE = 128
CHUNK = 128
VMEM_LIMIT_BYTES = 56 * 1024 * 1024
NEG_BIG = -1e30
ROW_BLOCK = 512
ROUTE_BLOCK = 256
EXPERT_BLOCK = 512
PAGES_PER_STEP = 4
ATTN_Q_BLOCK = 512
ATTN_K_BLOCK = 512
ADA_COL_BLOCK = 1024
CONV_COL_BLOCK = 512
CONV_ROW_BLOCK = 256


def _params(*sem):
    return pltpu.CompilerParams(dimension_semantics=sem, vmem_limit_bytes=VMEM_LIMIT_BYTES)


def _modnorm(x, g, shift, scale):
    y = x * lax.rsqrt(jnp.mean(x * x, axis=-1, keepdims=True) + EPS)
    return (y * g) * (1.0 + scale) + shift


def _gelu(a):
    return jax.nn.gelu(a)


def _log_sigmoid(x):
    return jnp.minimum(x, 0.0) - jnp.log1p(jnp.exp(-jnp.abs(x)))


def _split3(x):
    hi = x.astype(BF16)
    r = x - hi.astype(F32)
    mid = r.astype(BF16)
    lo = (r - mid.astype(F32)).astype(BF16)
    return hi, mid, lo


def _dot(a, b):
    return jnp.dot(a, b, preferred_element_type=F32)


def _dot_nt(a, b):
    return lax.dot_general(a, b, (((1,), (1,)), ((), ())), preferred_element_type=F32)


def _mod_spec(mod, blocks_per_group):
    _, r, d = mod.shape
    if r == 1:
        return pl.BlockSpec((1, 1, d), lambda i, *_: (i // blocks_per_group, 0, 0))
    return pl.BlockSpec((1, r, d), lambda i, *_: (i, 0, 0))


def _ada_kernel(c_ref, w_ref, b_ref, o_ref):
    c = c_ref[...]
    s = (c * jax.nn.sigmoid(c)).astype(BF16)
    o_ref[0] = _dot(s, w_ref[0].astype(BF16)) + b_ref[0]


def _ada(c_all, w_ada, b_ada):
    depth, d, n6 = w_ada.shape
    r = c_all.shape[0]
    tn = ADA_COL_BLOCK
    return pl.pallas_call(
        _ada_kernel,
        grid=(depth, n6 // tn),
        in_specs=[pl.BlockSpec((r, d), lambda l, j: (0, 0)),
                  pl.BlockSpec((1, d, tn), lambda l, j: (l, 0, j)),
                  pl.BlockSpec((1, 1, tn), lambda l, j: (l, 0, j))],
        out_specs=pl.BlockSpec((1, r, tn), lambda l, j: (l, 0, j)),
        out_shape=jax.ShapeDtypeStruct((depth, r, n6), F32),
        compiler_params=_params("arbitrary", "arbitrary"),
        name="ada_mod",
    )(c_all, w_ada, b_ada.reshape(depth, 1, n6))


def _ag_in_kernel(x_ref, g_ref, sh_ref, sc_ref, w_ref, wf_ref, bf_ref, qg_ref, kg_ref,
                  q_ref, k_ref, v_ref, u_ref, gv_ref, lf_ref, h_scr, *, n_heads):
    j = pl.program_id(1)

    @pl.when(j == 0)
    def _():
        h = _modnorm(x_ref[...], g_ref[...], sh_ref[0], sc_ref[0]).astype(BF16)
        h_scr[...] = h
        lf_ref[...] = _log_sigmoid(_dot(h, wf_ref[...]) + bf_ref[...])

    def z():
        return _dot(h_scr[...], w_ref[...])

    def head_norm(zz, g, o_ref, post):
        for hd in range(n_heads):
            zh = zz[:, hd * HEAD_DIM:(hd + 1) * HEAD_DIM]
            y = zh * lax.rsqrt(jnp.mean(zh * zh, axis=-1, keepdims=True) + EPS) * g
            o_ref[:, hd * HEAD_DIM:(hd + 1) * HEAD_DIM] = (y * post).astype(o_ref.dtype)

    @pl.when(j == 0)
    def _():
        head_norm(z(), qg_ref[...], q_ref, HEAD_DIM ** -0.5)

    @pl.when(j == 1)
    def _():
        head_norm(z(), kg_ref[...], k_ref, 1.0)

    @pl.when(j == 2)
    def _():
        v_ref[...] = z()

    @pl.when(j == 3)
    def _():
        u_ref[...] = _gelu(z()).astype(u_ref.dtype)

    @pl.when(j == 4)
    def _():
        gv_ref[...] = _gelu(z())


def _ag_in(x, g, shift, scale, w5, wf, bf, qg, kg, *, blocks_per_group, q_dtype):
    n, d = x.shape
    w = w5.shape[1] // 5
    tm = ROW_BLOCK
    row = lambda i, j: (i, 0)
    const = lambda i, j: (0, 0)
    out_shapes = (jax.ShapeDtypeStruct((n, w), q_dtype),
                  jax.ShapeDtypeStruct((n, w), F32),
                  jax.ShapeDtypeStruct((n, w), F32),
                  jax.ShapeDtypeStruct((n, w), BF16),
                  jax.ShapeDtypeStruct((n, w), F32),
                  jax.ShapeDtypeStruct((n, LANES), F32))
    return pl.pallas_call(
        functools.partial(_ag_in_kernel, n_heads=w // HEAD_DIM),
        grid=(n // tm, 5),
        in_specs=[pl.BlockSpec((tm, d), row),
                  pl.BlockSpec((1, d), const),
                  _mod_spec(shift, blocks_per_group),
                  _mod_spec(scale, blocks_per_group),
                  pl.BlockSpec((d, w), lambda i, j: (0, j)),
                  pl.BlockSpec((d, LANES), const),
                  pl.BlockSpec((1, LANES), const),
                  pl.BlockSpec((1, HEAD_DIM), const),
                  pl.BlockSpec((1, HEAD_DIM), const)],
        out_specs=[pl.BlockSpec((tm, w), row)] * 5 + [pl.BlockSpec((tm, LANES), row)],
        out_shape=out_shapes,
        scratch_shapes=[pltpu.VMEM((tm, d), BF16)],
        compiler_params=_params("arbitrary", "arbitrary"),
        name="ag_in",
    )(x, g, shift, scale, w5, wf, bf, qg, kg)


def _tri_ones(n, strict):
    r = lax.broadcasted_iota(jnp.int32, (n, n), 0)
    c = lax.broadcasted_iota(jnp.int32, (n, n), 1)
    return jnp.where((r > c) if strict else (r >= c), 1.0, 0.0).astype(BF16)


def _cumsum_kernel(lf_ref, f_ref, ft_ref):
    t = lf_ref.shape[1]
    tri = _tri_ones(CHUNK, strict=False)
    carry = jnp.zeros((1, LANES), F32)
    for c in range(t // CHUNK):
        hi, mid, lo = _split3(lf_ref[0, c * CHUNK:(c + 1) * CHUNK, :])
        cs = (_dot(tri, hi) + _dot(tri, mid)) + _dot(tri, lo) + carry
        f_ref[0, c * CHUNK:(c + 1) * CHUNK, :] = cs
        carry = cs[CHUNK - 1:CHUNK, :]
    ft_ref[0] = f_ref[0].T


def _cumsum(lf):
    b, t, _ = lf.shape
    return pl.pallas_call(
        _cumsum_kernel,
        grid=(b,),
        in_specs=[pl.BlockSpec((1, t, LANES), lambda i: (i, 0, 0))],
        out_specs=[pl.BlockSpec((1, t, LANES), lambda i: (i, 0, 0)),
                   pl.BlockSpec((1, LANES, t), lambda i: (i, 0, 0))],
        out_shape=(jax.ShapeDtypeStruct((b, t, LANES), F32), jax.ShapeDtypeStruct((b, LANES, t), F32)),
        compiler_params=_params("arbitrary"),
        name="fox_cumsum",
    )(lf)


def _attn_kernel(q_ref, k_ref, v_ref, f_ref, ft_ref, o_ref, *, tq, tk):
    hd = pl.program_id(1)
    i = pl.program_id(2)
    q = q_ref[0]
    lane = lax.broadcasted_iota(jnp.int32, (tq, LANES), 1)
    fq = jnp.sum(jnp.where(lane == hd, f_ref[0], 0.0), axis=-1, keepdims=True)
    row = i * tq + lax.broadcasted_iota(jnp.int32, (tq, tk), 0)
    col = lax.broadcasted_iota(jnp.int32, (tq, tk), 1)
    n_kb = ((i + 1) * tq + tk - 1) // tk

    def body(j, carry):
        m, l, acc = carry
        start = pl.multiple_of(j * tk, tk)
        ks = k_ref[0, pl.ds(start, tk), :].astype(BF16)
        vs = v_ref[0, pl.ds(start, tk), :].astype(BF16)
        fk = ft_ref[0, hd, pl.ds(j, 1), :]
        s = _dot_nt(q, ks) + fq - fk
        s = jnp.where(col + j * tk <= row, s, NEG_BIG)
        m_new = jnp.maximum(m, jnp.max(s, axis=-1, keepdims=True))
        alpha = jnp.exp(m - m_new)
        p = jnp.exp(s - m_new)
        l = alpha * l + jnp.sum(p, axis=-1, keepdims=True)
        acc = alpha * acc + _dot(p.astype(BF16), vs)
        return m_new, l, acc

    m0 = jnp.full((tq, 1), NEG_BIG, F32)
    l0 = jnp.zeros((tq, 1), F32)
    a0 = jnp.zeros((tq, HEAD_DIM), F32)
    _, l, acc = lax.fori_loop(0, n_kb, body, (m0, l0, a0))
    o_ref[0] = (acc / l).astype(o_ref.dtype)


def _attn_prompt(q, k, v, f, ft4):
    b, t, w = k.shape
    n_heads = w // HEAD_DIM
    tq, tk = min(ATTN_Q_BLOCK, t), min(ATTN_K_BLOCK, t)
    return pl.pallas_call(
        functools.partial(_attn_kernel, tq=tq, tk=tk),
        grid=(b, n_heads, t // tq),
        in_specs=[pl.BlockSpec((1, tq, HEAD_DIM), lambda bi, h, i: (bi, i, h)),
                  pl.BlockSpec((1, t, HEAD_DIM), lambda bi, h, i: (bi, 0, h)),
                  pl.BlockSpec((1, t, HEAD_DIM), lambda bi, h, i: (bi, 0, h)),
                  pl.BlockSpec((1, tq, LANES), lambda bi, h, i: (bi, i, 0)),
                  pl.BlockSpec((1, n_heads, t // tk, tk), lambda bi, h, i: (bi, 0, 0, 0))],
        out_specs=pl.BlockSpec((1, tq, HEAD_DIM), lambda bi, h, i: (bi, i, h)),
        out_shape=jax.ShapeDtypeStruct((b, t, w), BF16),
        compiler_params=_params("arbitrary", "arbitrary", "arbitrary"),
        name="fox_prompt",
    )(q, k, v, f, ft4)


def _attn_sample_kernel(pt_ref, q_ref, kn_ref, vn_ref, lfn_ref, *rest, n_heads, pages_per_step):
    npg = pages_per_step
    kp_refs = rest[:npg]
    vp_refs = rest[npg:2 * npg]
    lp_refs = rest[2 * npg:3 * npg]
    o_ref = rest[3 * npg]
    qbd_scr, m_scr, l_scr, acc_scr, carry_scr, colb_scr = rest[3 * npg + 1:]
    g = pl.program_id(1)
    n_g = pl.num_programs(1)
    nq = q_ref.shape[1]
    w = q_ref.shape[2]
    nr = nq * n_heads
    rid = lax.broadcasted_iota(jnp.int32, (nr, LANES), 0)
    lid = lax.broadcasted_iota(jnp.int32, (nr, LANES), 1)
    head_sel = lid == (rid % n_heads)

    def cnew():
        c = lfn_ref[0]
        r8 = lax.broadcasted_iota(jnp.int32, c.shape, 0)
        sft = 1
        while sft < nq:
            c = c + jnp.where(r8 >= sft, pltpu.roll(c, sft, axis=0), 0.0)
            sft *= 2
        return c

    def online(s, vb):
        m_old = m_scr[...]
        m_new = jnp.maximum(m_old, jnp.max(s, axis=-1, keepdims=True))
        alpha = jnp.exp(m_old - m_new)
        p = jnp.exp(s - m_new)
        l_scr[...] = alpha * l_scr[...] + jnp.sum(p, axis=-1, keepdims=True)
        acc_scr[...] = alpha * acc_scr[...] + _dot(p.astype(BF16), vb)
        m_scr[...] = m_new

    @pl.when(g == 0)
    def _():
        q = q_ref[0]
        qt = jnp.concatenate([jnp.broadcast_to(q[t:t + 1, :], (n_heads, w)) for t in range(nq)], axis=0)
        rw = lax.broadcasted_iota(jnp.int32, (nr, w), 0)
        lw = lax.broadcasted_iota(jnp.int32, (nr, w), 1)
        qbd_scr[...] = jnp.where((lw // HEAD_DIM) == (rw % n_heads), qt, 0.0).astype(BF16)
        m_scr[...] = jnp.full(m_scr.shape, NEG_BIG, F32)
        l_scr[...] = jnp.zeros(l_scr.shape, F32)
        acc_scr[...] = jnp.zeros(acc_scr.shape, F32)
        carry_scr[...] = jnp.zeros(carry_scr.shape, F32)
        c = cnew()
        cexp = jnp.concatenate([jnp.broadcast_to(c[t:t + 1, :], (n_heads, LANES)) for t in range(nq)], axis=0)
        colb_scr[...] = jnp.sum(jnp.where(head_sel, cexp, 0.0), axis=-1, keepdims=True)

    su = _tri_ones(PAGE_SIZE, strict=True)
    lf_all = jnp.concatenate([lp_refs[r][0, 0] for r in range(npg)], axis=0)
    hi, mid, lo = _split3(lf_all)
    r_all = (_dot(hi, su) + _dot(mid, su)) + _dot(lo, su)
    colb = colb_scr[...]
    qbd = qbd_scr[...]
    for r in range(npg):
        lf_p = lf_all[r * n_heads:(r + 1) * n_heads, :]
        carry = carry_scr[...]
        r_p = r_all[r * n_heads:(r + 1) * n_heads, :] + carry
        carry_scr[...] = r_p[:, 0:1] + lf_p[:, 0:1]
        bias = jnp.concatenate([r_p] * nq, axis=0)
        kb = kp_refs[r][0, 0].astype(BF16)
        vb = vp_refs[r][0, 0].astype(BF16)
        online(_dot_nt(qbd, kb) + colb + bias, vb)

    @pl.when(g == n_g - 1)
    def _():
        c = cnew()
        cpad = jnp.concatenate([c, jnp.zeros((LANES - nq, LANES), F32)], axis=0)
        ct = cpad.T[0:n_heads, :]
        dtile = jnp.concatenate([ct] * nq, axis=0)
        kb = jnp.concatenate([kn_ref[0], jnp.zeros((PAGE_SIZE - nq, w), F32)], axis=0).astype(BF16)
        vb = jnp.concatenate([vn_ref[0], jnp.zeros((PAGE_SIZE - nq, w), F32)], axis=0).astype(BF16)
        s = _dot_nt(qbd, kb) + colb - dtile
        s = jnp.where(lid <= rid // n_heads, s, NEG_BIG)
        online(s, vb)
        rw = lax.broadcasted_iota(jnp.int32, (nr, w), 0)
        lw = lax.broadcasted_iota(jnp.int32, (nr, w), 1)
        o = jnp.where((lw // HEAD_DIM) == (rw % n_heads), acc_scr[...] / l_scr[...], 0.0)
        rows = [jnp.sum(o[t * n_heads:(t + 1) * n_heads, :], axis=0, keepdims=True) for t in range(nq)]
        o_ref[0] = jnp.concatenate(rows, axis=0).astype(o_ref.dtype)


def _attn_sample(page_table, q, kn, vn, lfn, cache_k, cache_v, cache_lft, layer):
    nb, nq, w = q.shape
    n_heads = w // HEAD_DIM
    n_pages = page_table.shape[1]
    npg = PAGES_PER_STEP
    while n_pages % npg:
        npg //= 2
    assert nq == SUBLANES and n_heads == SUBLANES

    def page_spec(r, last):
        def imap(b, g, pt):
            return (layer, pt[b, n_pages - 1 - (g * npg + r)], 0, 0)
        return pl.BlockSpec((1, 1) + last, imap)

    new_spec = lambda last: pl.BlockSpec((1, nq, last), lambda b, g, pt: (b, 0, 0))
    in_specs = ([new_spec(w), new_spec(w), new_spec(w), new_spec(LANES)]
                + [page_spec(r, (PAGE_SIZE, w)) for r in range(npg)]
                + [page_spec(r, (PAGE_SIZE, w)) for r in range(npg)]
                + [page_spec(r, (n_heads, PAGE_SIZE)) for r in range(npg)])
    nr = nq * n_heads
    grid_spec = pltpu.PrefetchScalarGridSpec(
        num_scalar_prefetch=1,
        grid=(nb, n_pages // npg),
        in_specs=in_specs,
        out_specs=pl.BlockSpec((1, nq, w), lambda b, g, pt: (b, 0, 0)),
        scratch_shapes=[pltpu.VMEM((nr, w), BF16), pltpu.VMEM((nr, 1), F32), pltpu.VMEM((nr, 1), F32),
                        pltpu.VMEM((nr, w), F32), pltpu.VMEM((n_heads, 1), F32), pltpu.VMEM((nr, 1), F32)])
    return pl.pallas_call(
        functools.partial(_attn_sample_kernel, n_heads=n_heads, pages_per_step=npg),
        grid_spec=grid_spec,
        out_shape=jax.ShapeDtypeStruct((nb, nq, w), BF16),
        compiler_params=_params("arbitrary", "arbitrary"),
        name="fox_sample",
    )(page_table, q, kn, vn, lfn, *([cache_k] * npg), *([cache_v] * npg), *([cache_lft] * npg))


def _ag_out_kernel(attn_ref, u_ref, gv_ref, mix_ref, bias_ref, wa_ref, wg_ref, x_ref, gate_ref,
                   o_ref, gm_scr, *, n_groups, chunk):
    tm = x_ref.shape[0]
    for g in range(n_groups):
        cols = slice(g * LANES, (g + 1) * LANES)
        for c in range(tm // chunk):
            rows = slice(c * chunk, (c + 1) * chunk)
            s = _dot(mix_ref[g], gv_ref[rows, cols].astype(BF16)) + bias_ref[rows, cols]
            gm_scr[rows, cols] = (u_ref[rows, cols].astype(F32) * s).astype(BF16)
    out = _dot(attn_ref[...], wa_ref[...]) + _dot(gm_scr[...], wg_ref[...])
    o_ref[...] = x_ref[...] + gate_ref[0] * out


def _ag_out(attn, u, gv, mix, bias, wa, wg, x, gate, *, blocks_per_group):
    n, d = x.shape
    w = attn.shape[1]
    tm = ROW_BLOCK
    n_groups, chunk, _ = mix.shape
    row = lambda i: (i, 0)
    const2 = lambda i: (0, 0)
    return pl.pallas_call(
        functools.partial(_ag_out_kernel, n_groups=n_groups, chunk=chunk),
        grid=(n // tm,),
        in_specs=[pl.BlockSpec((tm, w), row), pl.BlockSpec((tm, w), row), pl.BlockSpec((tm, w), row),
                  pl.BlockSpec(mix.shape, lambda i: (0, 0, 0)),
                  pl.BlockSpec((tm, w), const2),
                  pl.BlockSpec((w, d), const2), pl.BlockSpec((w, d), const2),
                  pl.BlockSpec((tm, d), row),
                  _mod_spec(gate, blocks_per_group)],
        out_specs=pl.BlockSpec((tm, d), row),
        out_shape=jax.ShapeDtypeStruct((n, d), F32),
        scratch_shapes=[pltpu.VMEM((tm, w), BF16)],
        compiler_params=_params("arbitrary"),
        name="ag_out",
    )(attn, u, gv, mix, bias, wa, wg, x, gate)


def _conv_in_kernel(x_ref, g_ref, sh_ref, sc_ref, wb_ref, wc_ref, wz_ref, bg_ref, p_ref, h_scr):
    @pl.when(pl.program_id(1) == 0)
    def _():
        h_scr[...] = _modnorm(x_ref[...], g_ref[...], sh_ref[0], sc_ref[0]).astype(BF16)

    h = h_scr[...]
    bg_ref[...] = _dot(h, wb_ref[...]).astype(bg_ref.dtype)
    p_ref[...] = _dot(h, wc_ref[...]) * _dot(h, wz_ref[...])


def _conv_in(x, g, shift, scale, w3, *, blocks_per_group):
    n, d = x.shape
    tm, tn = ROW_BLOCK, CONV_COL_BLOCK
    nj = d // tn
    row = lambda i, j: (i, 0)
    return pl.pallas_call(
        _conv_in_kernel,
        grid=(n // tm, nj),
        in_specs=[pl.BlockSpec((tm, d), row),
                  pl.BlockSpec((1, d), lambda i, j: (0, 0)),
                  _mod_spec(shift, blocks_per_group),
                  _mod_spec(scale, blocks_per_group),
                  pl.BlockSpec((d, tn), lambda i, j: (0, j)),
                  pl.BlockSpec((d, tn), lambda i, j: (0, j + nj)),
                  pl.BlockSpec((d, tn), lambda i, j: (0, j + 2 * nj))],
        out_specs=[pl.BlockSpec((tm, tn), lambda i, j: (i, j))] * 2,
        out_shape=(jax.ShapeDtypeStruct((n, d), BF16), jax.ShapeDtypeStruct((n, d), F32)),
        scratch_shapes=[pltpu.VMEM((tm, d), BF16)],
        compiler_params=_params("arbitrary", "arbitrary"),
        name="conv_in",
    )(x, g, shift, scale, w3, w3, w3)


def _conv_out_kernel(p_ref, s1_ref, s2_ref, bg_ref, cw_ref, cb_ref, wo_ref, x_ref, gate_ref, o_ref, *, period):
    p = p_ref[...]
    t = lax.broadcasted_iota(jnp.int32, p.shape, 0) % period
    if period == p.shape[0]:
        s1 = jnp.broadcast_to(s1_ref[0, 0:1, :], p.shape)
        s2 = jnp.where(t == 0, jnp.broadcast_to(s2_ref[0, 0:1, :], p.shape),
                       jnp.broadcast_to(s2_ref[0, 1:2, :], p.shape))
    else:
        s1, s2 = s1_ref[...], s2_ref[...]
    p1 = jnp.where(t == 0, s1, pltpu.roll(p, 1, axis=0))
    p2 = jnp.where(t < 2, s2, pltpu.roll(p, 2, axis=0))
    y = cb_ref[...] + cw_ref[0:1, :] * p2
    y = y + cw_ref[1:2, :] * p1
    y = y + cw_ref[2:3, :] * p
    out = _dot((bg_ref[...].astype(F32) * y).astype(BF16), wo_ref[...])
    o_ref[...] = x_ref[...] + gate_ref[0] * out


def _conv_out(p, s1, s2, bg, cw, cb, wo, x, gate, *, blocks_per_group, period):
    n, d = x.shape
    tm = CONV_ROW_BLOCK
    row = lambda i: (i, 0)
    const2 = lambda i: (0, 0)
    if period == tm:
        s_spec = pl.BlockSpec((1, SUBLANES, d), lambda i: (i, 0, 0))
    else:
        s_spec = pl.BlockSpec((tm, d), row)
    return pl.pallas_call(
        functools.partial(_conv_out_kernel, period=period),
        grid=(n // tm,),
        in_specs=[pl.BlockSpec((tm, d), row), s_spec, s_spec, pl.BlockSpec((tm, d), row),
                  pl.BlockSpec(cw.shape, const2), pl.BlockSpec((1, d), const2),
                  pl.BlockSpec((d, d), const2), pl.BlockSpec((tm, d), row),
                  _mod_spec(gate, blocks_per_group)],
        out_specs=pl.BlockSpec((tm, d), row),
        out_shape=jax.ShapeDtypeStruct((n, d), F32),
        compiler_params=_params("arbitrary"),
        name="conv_out",
    )(p, s1, s2, bg, cw, cb, wo, x, gate)


def _route_kernel(x_ref, g_ref, sh_ref, sc_ref, wqt_ref, k1_ref, k2_ref,
                  xt_ref, e1_ref, e2_ref, th_ref,
                  qt_scr, s1_scr, s2_scr, w1_scr, w2_scr, t1_scr, t2_scr, cand_scr, top_scr):
    neg_inf = -jnp.inf
    ht = _modnorm(x_ref[...], g_ref[...], sh_ref[0], sc_ref[0]).T.astype(BF16)
    xt_ref[...] = ht
    qt_scr[...] = _dot(wqt_ref[...], ht)
    d_key = 2 * N_KEYS

    def head_body(hd, _):
        base = pl.multiple_of(hd * d_key, d_key)
        s1 = _dot(k1_ref[...], qt_scr[pl.ds(base, N_KEYS), :].astype(BF16))
        s2 = _dot(k2_ref[...], qt_scr[pl.ds(base + N_KEYS, N_KEYS), :].astype(BF16))
        s1_scr[...] = s1
        s2_scr[...] = s2
        w1_scr[...] = s1
        w2_scr[...] = s2

        def top_body(k, _):
            a = w1_scr[...]
            b = w2_scr[...]
            ma = jnp.max(a, axis=0, keepdims=True)
            mb = jnp.max(b, axis=0, keepdims=True)
            t1_scr[pl.ds(k, 1), :] = ma
            t2_scr[pl.ds(k, 1), :] = mb
            w1_scr[...] = jnp.where(a == ma, neg_inf, a)
            w2_scr[...] = jnp.where(b == mb, neg_inf, b)
            return 0

        lax.fori_loop(0, PEER_TOPK, top_body, 0)
        t2_all = t2_scr[...]
        for a in range(PEER_TOPK):
            cand_scr[a * PEER_TOPK:(a + 1) * PEER_TOPK, :] = t1_scr[a:a + 1, :] + t2_all

        def cand_body(k, _):
            c = cand_scr[...]
            mc = jnp.max(c, axis=0, keepdims=True)
            top_scr[pl.ds(k, 1), :] = mc
            cand_scr[...] = jnp.where(c == mc, neg_inf, c)
            return 0

        lax.fori_loop(0, PEER_TOPK + 1, cand_body, 0)
        top = top_scr[0:PEER_TOPK, :]
        z = jnp.sum(jnp.exp(top - top[0:1, :]), axis=0, keepdims=True)
        tau = 0.5 * (top[PEER_TOPK - 1:PEER_TOPK, :] + top_scr[PEER_TOPK:PEER_TOPK + 1, :])
        s1 = s1_scr[...]
        s2 = s2_scr[...]
        t10 = t1_scr[0:1, :]
        t20 = t2_scr[0:1, :]
        e1_ref[hd] = jnp.where(w1_scr[...] == neg_inf, jnp.exp(s1 - t10) / z, 0.0)
        e2_ref[hd] = jnp.where(w2_scr[...] == neg_inf, jnp.exp(s2 - t20), 0.0)
        th_ref[hd] = jnp.exp((tau - t20) - s1)
        return 0

    lax.fori_loop(0, PEER_HEADS, head_body, 0)


def _route(x, g, shift, scale, wqt, k1, k2, *, blocks_per_group):
    n, d = x.shape
    tm = ROUTE_BLOCK
    qw = wqt.shape[0]
    row = lambda i: (i, 0)
    const2 = lambda i: (0, 0)
    fac = jax.ShapeDtypeStruct((PEER_HEADS, N_KEYS, n), F32)
    fac_spec = pl.BlockSpec((PEER_HEADS, N_KEYS, tm), lambda i: (0, 0, i))
    key_scr = pltpu.VMEM((N_KEYS, tm), F32)
    top_rows = 3 * SUBLANES
    return pl.pallas_call(
        _route_kernel,
        grid=(n // tm,),
        in_specs=[pl.BlockSpec((tm, d), row), pl.BlockSpec((1, d), const2),
                  _mod_spec(shift, blocks_per_group * (ROW_BLOCK // tm)),
                  _mod_spec(scale, blocks_per_group * (ROW_BLOCK // tm)),
                  pl.BlockSpec((qw, d), const2),
                  pl.BlockSpec((N_KEYS, N_KEYS), const2), pl.BlockSpec((N_KEYS, N_KEYS), const2)],
        out_specs=[pl.BlockSpec((d, tm), lambda i: (0, i)), fac_spec, fac_spec, fac_spec],
        out_shape=(jax.ShapeDtypeStruct((d, n), BF16), fac, fac, fac),
        scratch_shapes=[pltpu.VMEM((qw, tm), F32), key_scr, key_scr, key_scr, key_scr,
                        pltpu.VMEM((PEER_TOPK, tm), F32), pltpu.VMEM((PEER_TOPK, tm), F32),
                        pltpu.VMEM((PEER_TOPK * PEER_TOPK, tm), F32), pltpu.VMEM((top_rows, tm), F32)],
        compiler_params=_params("arbitrary"),
        name="peer_route",
    )(x, g, shift, scale, wqt, k1, k2)


def _peer_kernel(xt_ref, u_ref, vt_ref, e1_ref, e2_ref, th_ref, yt_ref, a_scr, h_scr):
    c = pl.program_id(1)
    te = u_ref.shape[0]

    @pl.when(c == 0)
    def _():
        yt_ref[...] = jnp.zeros(yt_ref.shape, F32)

    a_scr[...] = _dot(u_ref[...], xt_ref[...])
    for ii in range(te // N_KEYS):
        i_glob = c * (te // N_KEYS) + ii
        gate = None
        for hd in range(PEER_HEADS):
            e2 = e2_ref[hd]
            th = th_ref[hd, pl.ds(i_glob, 1), :]
            e1 = e1_ref[hd, pl.ds(i_glob, 1), :]
            term = jnp.where(e2 >= th, e2, 0.0) * e1
            gate = term if gate is None else gate + term
        rows = slice(ii * N_KEYS, (ii + 1) * N_KEYS)
        h_scr[rows, :] = (_gelu(a_scr[rows, :]) * gate).astype(BF16)
    yt_ref[...] += _dot(vt_ref[...], h_scr[...])


def _peer(xt, u, vt, e1, e2, th):
    d, n = xt.shape
    ne = u.shape[0]
    tm, te = ROW_BLOCK, EXPERT_BLOCK
    fac_spec = pl.BlockSpec((PEER_HEADS, N_KEYS, tm), lambda i, c: (0, 0, i))
    return pl.pallas_call(
        _peer_kernel,
        grid=(n // tm, ne // te),
        in_specs=[pl.BlockSpec((d, tm), lambda i, c: (0, i)),
                  pl.BlockSpec((te, d), lambda i, c: (c, 0)),
                  pl.BlockSpec((d, te), lambda i, c: (0, c)),
                  fac_spec, fac_spec, fac_spec],
        out_specs=pl.BlockSpec((d, tm), lambda i, c: (0, i)),
        out_shape=jax.ShapeDtypeStruct((d, n), F32),
        scratch_shapes=[pltpu.VMEM((te, tm), F32), pltpu.VMEM((te, tm), BF16)],
        compiler_params=_params("arbitrary", "arbitrary"),
        name="peer_dense",
    )(xt, u, vt, e1, e2, th)


def _resid_kernel(x_ref, yt_ref, gate_ref, o_ref):
    o_ref[...] = x_ref[...] + gate_ref[0] * yt_ref[...].T


def _resid(x, yt, gate, *, blocks_per_group):
    n, d = x.shape
    tm = ROW_BLOCK
    row = lambda i: (i, 0)
    return pl.pallas_call(
        _resid_kernel,
        grid=(n // tm,),
        in_specs=[pl.BlockSpec((tm, d), row), pl.BlockSpec((d, tm), lambda i: (0, i)),
                  _mod_spec(gate, blocks_per_group)],
        out_specs=pl.BlockSpec((tm, d), row),
        out_shape=jax.ShapeDtypeStruct((n, d), F32),
        compiler_params=_params("arbitrary"),
        name="peer_resid",
    )(x, yt, gate)


class _Stream:
    def __init__(self, x, seq):
        self.batch = x.shape[0]
        self.seq = seq
        self.x = x.reshape(-1, x.shape[-1])
        self.long = seq % ROW_BLOCK == 0
        self.blocks_per_group = seq // ROW_BLOCK if self.long else 1

    def mod(self, m, k, tm=ROW_BLOCK):
        mk = m[:, k, :]
        if self.long:
            return mk[:, None, :]
        d = mk.shape[-1]
        return jnp.repeat(mk, self.seq, axis=0).reshape(-1, tm, d)


def kernel(x_prompt, x_sample, c_prompt, c_sample, cache_k, cache_v, cache_logf, page_table, state_conv,
           w_ada, b_ada, norm_mix, norm_ffn, w_in_ag, b_fgate, q_norm, k_norm, w_spatial, b_spatial,
           w_out_ag, w_in_conv, conv_w, conv_b, w_out_conv, peer_wq, peer_keys, peer_u, peer_v):
    bp, tp, d = x_prompt.shape
    bs, ts, _ = x_sample.shape
    depth = w_ada.shape[0]
    att_w = d // 2
    n_heads = att_w // HEAD_DIM
    n_groups = att_w // LANES
    assert tp % ROW_BLOCK == 0 and (bs * ts) % ROW_BLOCK == 0 and ROW_BLOCK % ts == 0 and ts == SUBLANES

    streams = [_Stream(x_prompt, tp), _Stream(x_sample, ts)]
    mod_all = _ada(jnp.concatenate([c_prompt, c_sample], axis=0), w_ada, b_ada)
    n_pool = cache_k.shape[1]
    ck = cache_k.reshape(cache_k.shape[0], n_pool, PAGE_SIZE, att_w)
    cv = cache_v.reshape(cache_v.shape[0], n_pool, PAGE_SIZE, att_w)
    clt = jnp.swapaxes(cache_logf, 2, 3)

    outs = {"k": [[], []], "v": [[], []], "f": [[], []], "g": [], "c": [[], []]}
    for layer in range(depth):
        ml = mod_all[layer].reshape(bp + bs, N_MOD, d)
        mods = [ml[:bp], ml[bp:]]
        g_mix = norm_mix[layer][None, :]
        g_ffn = norm_ffn[layer][None, :]
        if layer % 2 == 0:
            a = layer // 2
            w_in = w_in_ag[a]
            w5 = jnp.concatenate([w_in[:, :3 * att_w], w_in[:, 3 * att_w + n_heads:]], axis=1).astype(BF16)
            wf = jnp.pad(w_in[:, 3 * att_w:3 * att_w + n_heads], ((0, 0), (0, LANES - n_heads))).astype(BF16)
            bf = jnp.pad(b_fgate[a], (0, LANES - n_heads))[None, :]
            wa = w_out_ag[a][:att_w].astype(BF16)
            wg = w_out_ag[a][att_w:].astype(BF16)
            tril = jnp.tril(w_spatial[a])
        else:
            ci = layer // 2
            w3 = w_in_conv[ci].astype(BF16)
            wo = w_out_conv[ci].astype(BF16)
        wqt = peer_wq[layer].T.astype(BF16)
        k1 = peer_keys[layer, 0].astype(BF16)
        k2 = peer_keys[layer, 1].astype(BF16)
        pu = peer_u[layer].astype(BF16)
        pvt = peer_v[layer].T.astype(BF16)

        for si, st in enumerate(streams):
            m = mods[si]
            bpg = st.blocks_per_group
            if layer % 2 == 0:
                q, k, v, u, gv, lf = _ag_in(st.x, g_mix, st.mod(m, 0), st.mod(m, 1), w5, wf, bf,
                                            q_norm[a][None, :], k_norm[a][None, :],
                                            blocks_per_group=bpg, q_dtype=BF16 if st.long else F32)
                if st.long:
                    f, ft = _cumsum(lf.reshape(st.batch, st.seq, LANES))
                    tk = min(ATTN_K_BLOCK, st.seq)
                    ft4 = ft[:, :n_heads, :].reshape(st.batch, n_heads, st.seq // tk, tk)
                    attn = _attn_prompt(q.reshape(st.batch, st.seq, att_w), k.reshape(st.batch, st.seq, att_w),
                                        v.reshape(st.batch, st.seq, att_w), f, ft4)
                    mix = tril.astype(BF16)
                    bias = jnp.tile(jnp.repeat(b_spatial[a].T, LANES, axis=1), (ROW_BLOCK // CHUNK, 1))
                else:
                    attn = _attn_sample(page_table, q.reshape(st.batch, st.seq, att_w),
                                        k.reshape(st.batch, st.seq, att_w), v.reshape(st.batch, st.seq, att_w),
                                        lf.reshape(st.batch, st.seq, LANES), ck, cv, clt, a)
                    eye = jnp.eye(ROW_BLOCK // st.seq, dtype=F32)
                    mix = jax.vmap(lambda t: jnp.kron(eye, t))(tril[:, :st.seq, :st.seq]).astype(BF16)
                    bias = jnp.tile(jnp.repeat(b_spatial[a].T[:st.seq], LANES, axis=1), (ROW_BLOCK // st.seq, 1))
                    outs["g"].append(gv.reshape(st.batch, st.seq, att_w))
                st.x = _ag_out(attn.reshape(-1, att_w), u, gv, mix, bias, wa, wg, st.x, st.mod(m, 2),
                               blocks_per_group=bpg)
                outs["k"][si].append(k.reshape(st.batch, st.seq, n_heads, HEAD_DIM))
                outs["v"][si].append(v.reshape(st.batch, st.seq, n_heads, HEAD_DIM))
                outs["f"][si].append(lf[:, :n_heads].reshape(st.batch, st.seq, n_heads))
            else:
                bg, p = _conv_in(st.x, g_mix, st.mod(m, 0), st.mod(m, 1), w3, blocks_per_group=bpg)
                p3 = p.reshape(st.batch, st.seq, d)
                tc = CONV_ROW_BLOCK
                if st.long:
                    cb = st.seq // tc
                    pb = p.reshape(st.batch, cb, tc, d)[:, :, tc - SUBLANES:, :]
                    prev = jnp.concatenate([jnp.zeros_like(pb[:, :1]), pb[:, :-1]], axis=1)
                    prev = prev.reshape(st.batch * cb, SUBLANES, d)
                    s2 = prev[:, SUBLANES - 2:, :]
                    s2 = jnp.pad(s2, ((0, 0), (0, SUBLANES - 2), (0, 0)))
                    s1 = jnp.pad(prev[:, SUBLANES - 1:, :], ((0, 0), (0, SUBLANES - 1), (0, 0)))
                    period = tc
                else:
                    cb = 1
                    buf = state_conv[ci]
                    s2 = jnp.pad(buf, ((0, 0), (0, st.seq - 2), (0, 0))).reshape(-1, d)
                    s1 = jnp.pad(buf[:, 1:], ((0, 0), (0, st.seq - 1), (0, 0))).reshape(-1, d)
                    period = st.seq
                st.x = _conv_out(p, s1, s2, bg, conv_w[ci], conv_b[ci][None, :], wo, st.x, st.mod(m, 2, tc),
                                 blocks_per_group=cb, period=period)
                outs["c"][si].append(p3[:, st.seq - 2:, :])
            xt, e1, e2, th = _route(st.x, g_ffn, st.mod(m, 3, ROUTE_BLOCK), st.mod(m, 4, ROUTE_BLOCK),
                                    wqt, k1, k2, blocks_per_group=bpg)
            yt = _peer(xt, pu, pvt, e1, e2, th)
            st.x = _resid(st.x, yt, st.mod(m, 5), blocks_per_group=bpg)

    sp, ss = streams
    return (sp.x.reshape(bp, tp, d), ss.x.reshape(bs, ts, d),
            jnp.stack(outs["k"][0]), jnp.stack(outs["v"][0]), jnp.stack(outs["f"][0]),
            jnp.stack(outs["k"][1]), jnp.stack(outs["v"][1]), jnp.stack(outs["f"][1]),
            jnp.stack(outs["g"]), jnp.stack(outs["c"][0]), jnp.stack(outs["c"][1]))
```

```python
import functools

import jax
import jax.numpy as jnp
from jax import lax
from jax.experimental import pallas as pl
from jax.experimental.pallas import tpu as pltpu

F32 = jnp.float32
BF16 = jnp.bfloat16

EPS = 1e-6
LANES = 128
SUBLANES = 8
HEAD_DIM = 128
N_MOD = 6
PEER_HEADS = 8
PEER_TOPK = 16
N_KEYS = 128
PAGE_SIZE = 128
CHUNK = 128
VMEM_LIMIT_BYTES = 56 * 1024 * 1024
NEG_BIG = -1e30
ROW_BLOCK = 512
ROUTE_BLOCK = 256
EXPERT_BLOCK = 512
PAGES_PER_STEP = 8
ATTN_Q_BLOCK = 512
ATTN_K_BLOCK = 512
ADA_COL_BLOCK = 1024
CONV_COL_BLOCK = 512
CONV_ROW_BLOCK = 256
PEER_SCHED_FLAGS = None


def _params(*sem, flags=None):
    return pltpu.CompilerParams(dimension_semantics=sem, vmem_limit_bytes=VMEM_LIMIT_BYTES, flags=flags)


def _modnorm(x, g, shift, scale):
    y = x * lax.rsqrt(jnp.mean(x * x, axis=-1, keepdims=True) + EPS)
    return (y * g) * (1.0 + scale) + shift


def _gelu(a):
    return jax.nn.gelu(a)


def _log_sigmoid(x):
    return jnp.minimum(x, 0.0) - jnp.log1p(jnp.exp(-jnp.abs(x)))


def _split3(x):
    hi = x.astype(BF16)
    r = x - hi.astype(F32)
    mid = r.astype(BF16)
    lo = (r - mid.astype(F32)).astype(BF16)
    return hi, mid, lo


def _dot(a, b):
    return jnp.dot(a, b, preferred_element_type=F32)


def _dot_nt(a, b):
    return lax.dot_general(a, b, (((1,), (1,)), ((), ())), preferred_element_type=F32)


def _mod_spec(mod, blocks_per_group):
    _, r, d = mod.shape
    if r == 1:
        return pl.BlockSpec((1, 1, d), lambda i, *_: (i // blocks_per_group, 0, 0))
    return pl.BlockSpec((1, r, d), lambda i, *_: (i, 0, 0))


def _ada_kernel(c_ref, w_ref, b_ref, o_ref):
    c = c_ref[...]
    s = (c * jax.nn.sigmoid(c)).astype(BF16)
    o_ref[0] = _dot(s, w_ref[0].astype(BF16)) + b_ref[0]


def _ada(c_all, w_ada, b_ada):
    depth, d, n6 = w_ada.shape
    r = c_all.shape[0]
    tn = ADA_COL_BLOCK
    return pl.pallas_call(
        _ada_kernel,
        grid=(depth, n6 // tn),
        in_specs=[pl.BlockSpec((r, d), lambda l, j: (0, 0)),
                  pl.BlockSpec((1, d, tn), lambda l, j: (l, 0, j)),
                  pl.BlockSpec((1, 1, tn), lambda l, j: (l, 0, j))],
        out_specs=pl.BlockSpec((1, r, tn), lambda l, j: (l, 0, j)),
        out_shape=jax.ShapeDtypeStruct((depth, r, n6), F32),
        compiler_params=_params("arbitrary", "arbitrary"),
        name="ada_mod",
    )(c_all, w_ada, b_ada.reshape(depth, 1, n6))


def _ag_in_kernel(x_ref, g_ref, sh_ref, sc_ref, w_ref, wf_ref, bf_ref, qg_ref, kg_ref,
                  q_ref, k_ref, v_ref, u_ref, gv_ref, lf_ref, h_scr, *, n_heads):
    j = pl.program_id(1)

    @pl.when(j == 0)
    def _():
        h = _modnorm(x_ref[...], g_ref[...], sh_ref[0], sc_ref[0]).astype(BF16)
        h_scr[...] = h
        lf_ref[...] = _log_sigmoid(_dot(h, wf_ref[...]) + bf_ref[...])

    def z():
        return _dot(h_scr[...], w_ref[...])

    def head_norm(zz, g, o_ref, post):
        for hd in range(n_heads):
            zh = zz[:, hd * HEAD_DIM:(hd + 1) * HEAD_DIM]
            y = zh * lax.rsqrt(jnp.mean(zh * zh, axis=-1, keepdims=True) + EPS) * g
            o_ref[:, hd * HEAD_DIM:(hd + 1) * HEAD_DIM] = (y * post).astype(o_ref.dtype)

    @pl.when(j == 0)
    def _():
        head_norm(z(), qg_ref[...], q_ref, HEAD_DIM ** -0.5)

    @pl.when(j == 1)
    def _():
        head_norm(z(), kg_ref[...], k_ref, 1.0)

    @pl.when(j == 2)
    def _():
        v_ref[...] = z()

    @pl.when(j == 3)
    def _():
        u_ref[...] = _gelu(z()).astype(u_ref.dtype)

    @pl.when(j == 4)
    def _():
        gv_ref[...] = _gelu(z())


def _ag_in(x, g, shift, scale, w5, wf, bf, qg, kg, *, blocks_per_group, q_dtype):
    n, d = x.shape
    w = w5.shape[1] // 5
    tm = ROW_BLOCK
    row = lambda i, j: (i, 0)
    const = lambda i, j: (0, 0)
    out_shapes = (jax.ShapeDtypeStruct((n, w), q_dtype),
                  jax.ShapeDtypeStruct((n, w), F32),
                  jax.ShapeDtypeStruct((n, w), F32),
                  jax.ShapeDtypeStruct((n, w), BF16),
                  jax.ShapeDtypeStruct((n, w), F32),
                  jax.ShapeDtypeStruct((n, LANES), F32))
    return pl.pallas_call(
        functools.partial(_ag_in_kernel, n_heads=w // HEAD_DIM),
        grid=(n // tm, 5),
        in_specs=[pl.BlockSpec((tm, d), row),
                  pl.BlockSpec((1, d), const),
                  _mod_spec(shift, blocks_per_group),
                  _mod_spec(scale, blocks_per_group),
                  pl.BlockSpec((d, w), lambda i, j: (0, j)),
                  pl.BlockSpec((d, LANES), const),
                  pl.BlockSpec((1, LANES), const),
                  pl.BlockSpec((1, HEAD_DIM), const),
                  pl.BlockSpec((1, HEAD_DIM), const)],
        out_specs=[pl.BlockSpec((tm, w), row)] * 5 + [pl.BlockSpec((tm, LANES), row)],
        out_shape=out_shapes,
        scratch_shapes=[pltpu.VMEM((tm, d), BF16)],
        compiler_params=_params("arbitrary", "arbitrary"),
        name="ag_in",
    )(x, g, shift, scale, w5, wf, bf, qg, kg)


def _tri_ones(n, strict):
    r = lax.broadcasted_iota(jnp.int32, (n, n), 0)
    c = lax.broadcasted_iota(jnp.int32, (n, n), 1)
    return jnp.where((r > c) if strict else (r >= c), 1.0, 0.0).astype(BF16)


def _cumsum_kernel(lf_ref, f_ref, ft_ref):
    t = lf_ref.shape[1]
    tri = _tri_ones(CHUNK, strict=False)
    carry = jnp.zeros((1, LANES), F32)
    for c in range(t // CHUNK):
        hi, mid, lo = _split3(lf_ref[0, c * CHUNK:(c + 1) * CHUNK, :])
        cs = (_dot(tri, hi) + _dot(tri, mid)) + _dot(tri, lo) + carry
        f_ref[0, c * CHUNK:(c + 1) * CHUNK, :] = cs
        carry = cs[CHUNK - 1:CHUNK, :]
    ft_ref[0] = f_ref[0].T


def _cumsum(lf):
    b, t, _ = lf.shape
    return pl.pallas_call(
        _cumsum_kernel,
        grid=(b,),
        in_specs=[pl.BlockSpec((1, t, LANES), lambda i: (i, 0, 0))],
        out_specs=[pl.BlockSpec((1, t, LANES), lambda i: (i, 0, 0)),
                   pl.BlockSpec((1, LANES, t), lambda i: (i, 0, 0))],
        out_shape=(jax.ShapeDtypeStruct((b, t, LANES), F32), jax.ShapeDtypeStruct((b, LANES, t), F32)),
        compiler_params=_params("arbitrary"),
        name="fox_cumsum",
    )(lf)


def _attn_kernel(q_ref, k_ref, v_ref, f_ref, ft_ref, o_ref, *, tq, tk):
    hd = pl.program_id(1)
    i = pl.program_id(2)
    q = q_ref[0]
    lane = lax.broadcasted_iota(jnp.int32, (tq, LANES), 1)
    fq = jnp.sum(jnp.where(lane == hd, f_ref[0], 0.0), axis=-1, keepdims=True)
    row = i * tq + lax.broadcasted_iota(jnp.int32, (tq, tk), 0)
    col = lax.broadcasted_iota(jnp.int32, (tq, tk), 1)
    n_kb = ((i + 1) * tq + tk - 1) // tk

    def body(j, carry):
        m, l, acc = carry
        start = pl.multiple_of(j * tk, tk)
        ks = k_ref[0, pl.ds(start, tk), :].astype(BF16)
        vs = v_ref[0, pl.ds(start, tk), :].astype(BF16)
        fk = ft_ref[0, hd, pl.ds(j, 1), :]
        s = _dot_nt(q, ks) + fq - fk
        s = jnp.where(col + j * tk <= row, s, NEG_BIG)
        m_new = jnp.maximum(m, jnp.max(s, axis=-1, keepdims=True))
        alpha = jnp.exp(m - m_new)
        p = jnp.exp(s - m_new)
        l = alpha * l + jnp.sum(p, axis=-1, keepdims=True)
        acc = alpha * acc + _dot(p.astype(BF16), vs)
        return m_new, l, acc

    m0 = jnp.full((tq, 1), NEG_BIG, F32)
    l0 = jnp.zeros((tq, 1), F32)
    a0 = jnp.zeros((tq, HEAD_DIM), F32)
    _, l, acc = lax.fori_loop(0, n_kb, body, (m0, l0, a0))
    o_ref[0] = (acc / l).astype(o_ref.dtype)


def _attn_prompt(q, k, v, f, ft4):
    b, t, w = k.shape
    n_heads = w // HEAD_DIM
    tq, tk = min(ATTN_Q_BLOCK, t), min(ATTN_K_BLOCK, t)
    return pl.pallas_call(
        functools.partial(_attn_kernel, tq=tq, tk=tk),
        grid=(b, n_heads, t // tq),
        in_specs=[pl.BlockSpec((1, tq, HEAD_DIM), lambda bi, h, i: (bi, i, h)),
                  pl.BlockSpec((1, t, HEAD_DIM), lambda bi, h, i: (bi, 0, h)),
                  pl.BlockSpec((1, t, HEAD_DIM), lambda bi, h, i: (bi, 0, h)),
                  pl.BlockSpec((1, tq, LANES), lambda bi, h, i: (bi, i, 0)),
                  pl.BlockSpec((1, n_heads, t // tk, tk), lambda bi, h, i: (bi, 0, 0, 0))],
        out_specs=pl.BlockSpec((1, tq, HEAD_DIM), lambda bi, h, i: (bi, i, h)),
        out_shape=jax.ShapeDtypeStruct((b, t, w), BF16),
        compiler_params=_params("arbitrary", "arbitrary", "arbitrary"),
        name="fox_prompt",
    )(q, k, v, f, ft4)


def _attn_sample_kernel(pt_ref, q_ref, kn_ref, vn_ref, lfn_ref, *rest, n_heads, pages_per_step):
    npg = pages_per_step
    kp_refs = rest[:npg]
    vp_refs = rest[npg:2 * npg]
    lp_refs = rest[2 * npg:3 * npg]
    o_ref = rest[3 * npg]
    qbd_scr, m_scr, l_scr, acc_scr, carry_scr, colb_scr = rest[3 * npg + 1:]
    g = pl.program_id(1)
    n_g = pl.num_programs(1)
    nq = q_ref.shape[1]
    w = q_ref.shape[2]
    nr = nq * n_heads
    rid = lax.broadcasted_iota(jnp.int32, (nr, LANES), 0)
    lid = lax.broadcasted_iota(jnp.int32, (nr, LANES), 1)
    head_sel = lid == (rid % n_heads)

    def cnew():
        c = lfn_ref[0]
        r8 = lax.broadcasted_iota(jnp.int32, c.shape, 0)
        sft = 1
        while sft < nq:
            c = c + jnp.where(r8 >= sft, pltpu.roll(c, sft, axis=0), 0.0)
            sft *= 2
        return c

    def online(s, vb):
        m_old = m_scr[...]
        m_new = jnp.maximum(m_old, jnp.max(s, axis=-1, keepdims=True))
        alpha = jnp.exp(m_old - m_new)
        p = jnp.exp(s - m_new)
        l_scr[...] = alpha * l_scr[...] + jnp.sum(p, axis=-1, keepdims=True)
        acc_scr[...] = alpha * acc_scr[...] + _dot(p.astype(BF16), vb)
        m_scr[...] = m_new

    @pl.when(g == 0)
    def _():
        q = q_ref[0]
        qt = jnp.concatenate([jnp.broadcast_to(q[t:t + 1, :], (n_heads, w)) for t in range(nq)], axis=0)
        rw = lax.broadcasted_iota(jnp.int32, (nr, w), 0)
        lw = lax.broadcasted_iota(jnp.int32, (nr, w), 1)
        qbd_scr[...] = jnp.where((lw // HEAD_DIM) == (rw % n_heads), qt, 0.0).astype(BF16)
        m_scr[...] = jnp.full(m_scr.shape, NEG_BIG, F32)
        l_scr[...] = jnp.zeros(l_scr.shape, F32)
        acc_scr[...] = jnp.zeros(acc_scr.shape, F32)
        carry_scr[...] = jnp.zeros(carry_scr.shape, F32)
        c = cnew()
        cexp = jnp.concatenate([jnp.broadcast_to(c[t:t + 1, :], (n_heads, LANES)) for t in range(nq)], axis=0)
        colb_scr[...] = jnp.sum(jnp.where(head_sel, cexp, 0.0), axis=-1, keepdims=True)

    su = _tri_ones(PAGE_SIZE, strict=True)
    lf_all = jnp.concatenate([lp_refs[r][0, 0] for r in range(npg)], axis=0)
    hi, mid, lo = _split3(lf_all)
    r_all = (_dot(hi, su) + _dot(mid, su)) + _dot(lo, su)
    colb = colb_scr[...]
    qbd = qbd_scr[...]
    for r in range(npg):
        lf_p = lf_all[r * n_heads:(r + 1) * n_heads, :]
        carry = carry_scr[...]
        r_p = r_all[r * n_heads:(r + 1) * n_heads, :] + carry
        carry_scr[...] = r_p[:, 0:1] + lf_p[:, 0:1]
        bias = jnp.concatenate([r_p] * nq, axis=0)
        kb = jnp.concatenate([kp_refs[r][0, 0, :, hd, :] for hd in range(n_heads)], axis=-1).astype(BF16)
        vb = jnp.concatenate([vp_refs[r][0, 0, :, hd, :] for hd in range(n_heads)], axis=-1).astype(BF16)
        online(_dot_nt(qbd, kb) + colb + bias, vb)

    @pl.when(g == n_g - 1)
    def _():
        c = cnew()
        cpad = jnp.concatenate([c, jnp.zeros((LANES - nq, LANES), F32)], axis=0)
        ct = cpad.T[0:n_heads, :]
        dtile = jnp.concatenate([ct] * nq, axis=0)
        kb = jnp.concatenate([kn_ref[0], jnp.zeros((PAGE_SIZE - nq, w), F32)], axis=0).astype(BF16)
        vb = jnp.concatenate([vn_ref[0], jnp.zeros((PAGE_SIZE - nq, w), F32)], axis=0).astype(BF16)
        s = _dot_nt(qbd, kb) + colb - dtile
        s = jnp.where(lid <= rid // n_heads, s, NEG_BIG)
        online(s, vb)
        rw = lax.broadcasted_iota(jnp.int32, (nr, w), 0)
        lw = lax.broadcasted_iota(jnp.int32, (nr, w), 1)
        o = jnp.where((lw // HEAD_DIM) == (rw % n_heads), acc_scr[...] / l_scr[...], 0.0)
        rows = [jnp.sum(o[t * n_heads:(t + 1) * n_heads, :], axis=0, keepdims=True) for t in range(nq)]
        o_ref[0] = jnp.concatenate(rows, axis=0).astype(o_ref.dtype)


def _attn_sample(page_table, q, kn, vn, lfn, cache_k, cache_v, cache_lft, layer):
    nb, nq, w = q.shape
    n_heads = w // HEAD_DIM
    n_pages = page_table.shape[1]
    npg = PAGES_PER_STEP
    while n_pages % npg:
        npg //= 2
    assert nq == SUBLANES and n_heads == SUBLANES

    def page_spec(r, last):
        def imap(b, g, pt):
            return (layer, pt[b, n_pages - 1 - (g * npg + r)]) + (0,) * len(last)
        return pl.BlockSpec((1, 1) + last, imap)

    new_spec = lambda last: pl.BlockSpec((1, nq, last), lambda b, g, pt: (b, 0, 0))
    in_specs = ([new_spec(w), new_spec(w), new_spec(w), new_spec(LANES)]
                + [page_spec(r, (PAGE_SIZE, n_heads, HEAD_DIM)) for r in range(npg)]
                + [page_spec(r, (PAGE_SIZE, n_heads, HEAD_DIM)) for r in range(npg)]
                + [page_spec(r, (n_heads, PAGE_SIZE)) for r in range(npg)])
    nr = nq * n_heads
    grid_spec = pltpu.PrefetchScalarGridSpec(
        num_scalar_prefetch=1,
        grid=(nb, n_pages // npg),
        in_specs=in_specs,
        out_specs=pl.BlockSpec((1, nq, w), lambda b, g, pt: (b, 0, 0)),
        scratch_shapes=[pltpu.VMEM((nr, w), BF16), pltpu.VMEM((nr, 1), F32), pltpu.VMEM((nr, 1), F32),
                        pltpu.VMEM((nr, w), F32), pltpu.VMEM((n_heads, 1), F32), pltpu.VMEM((nr, 1), F32)])
    return pl.pallas_call(
        functools.partial(_attn_sample_kernel, n_heads=n_heads, pages_per_step=npg),
        grid_spec=grid_spec,
        out_shape=jax.ShapeDtypeStruct((nb, nq, w), BF16),
        compiler_params=_params("arbitrary", "arbitrary"),
        name="fox_sample",
    )(page_table, q, kn, vn, lfn, *([cache_k] * npg), *([cache_v] * npg), *([cache_lft] * npg))


def _ag_out_kernel(attn_ref, u_ref, gv_ref, mix_ref, bias_ref, wa_ref, wg_ref, x_ref, gate_ref,
                   o_ref, gm_scr, *, n_groups, chunk):
    tm = x_ref.shape[0]
    for g in range(n_groups):
        cols = slice(g * LANES, (g + 1) * LANES)
        for c in range(tm // chunk):
            rows = slice(c * chunk, (c + 1) * chunk)
            s = _dot(mix_ref[g], gv_ref[rows, cols].astype(BF16)) + bias_ref[rows, cols]
            gm_scr[rows, cols] = (u_ref[rows, cols].astype(F32) * s).astype(BF16)
    out = _dot(attn_ref[...], wa_ref[...]) + _dot(gm_scr[...], wg_ref[...])
    o_ref[...] = x_ref[...] + gate_ref[0] * out


def _ag_out(attn, u, gv, mix, bias, wa, wg, x, gate, *, blocks_per_group):
    n, d = x.shape
    w = attn.shape[1]
    tm = ROW_BLOCK
    n_groups, chunk, _ = mix.shape
    row = lambda i: (i, 0)
    const2 = lambda i: (0, 0)
    return pl.pallas_call(
        functools.partial(_ag_out_kernel, n_groups=n_groups, chunk=chunk),
        grid=(n // tm,),
        in_specs=[pl.BlockSpec((tm, w), row), pl.BlockSpec((tm, w), row), pl.BlockSpec((tm, w), row),
                  pl.BlockSpec(mix.shape, lambda i: (0, 0, 0)),
                  pl.BlockSpec((tm, w), const2),
                  pl.BlockSpec((w, d), const2), pl.BlockSpec((w, d), const2),
                  pl.BlockSpec((tm, d), row),
                  _mod_spec(gate, blocks_per_group)],
        out_specs=pl.BlockSpec((tm, d), row),
        out_shape=jax.ShapeDtypeStruct((n, d), F32),
        scratch_shapes=[pltpu.VMEM((tm, w), BF16)],
        compiler_params=_params("arbitrary"),
        name="ag_out",
    )(attn, u, gv, mix, bias, wa, wg, x, gate)


def _conv_in_kernel(x_ref, g_ref, sh_ref, sc_ref, wb_ref, wc_ref, wz_ref, bg_ref, p_ref, h_scr):
    @pl.when(pl.program_id(1) == 0)
    def _():
        h_scr[...] = _modnorm(x_ref[...], g_ref[...], sh_ref[0], sc_ref[0]).astype(BF16)

    h = h_scr[...]
    bg_ref[...] = _dot(h, wb_ref[...]).astype(bg_ref.dtype)
    p_ref[...] = _dot(h, wc_ref[...]) * _dot(h, wz_ref[...])


def _conv_in(x, g, shift, scale, w3, *, blocks_per_group):
    n, d = x.shape
    tm, tn = ROW_BLOCK, CONV_COL_BLOCK
    nj = d // tn
    row = lambda i, j: (i, 0)
    return pl.pallas_call(
        _conv_in_kernel,
        grid=(n // tm, nj),
        in_specs=[pl.BlockSpec((tm, d), row),
                  pl.BlockSpec((1, d), lambda i, j: (0, 0)),
                  _mod_spec(shift, blocks_per_group),
                  _mod_spec(scale, blocks_per_group),
                  pl.BlockSpec((d, tn), lambda i, j: (0, j)),
                  pl.BlockSpec((d, tn), lambda i, j: (0, j + nj)),
                  pl.BlockSpec((d, tn), lambda i, j: (0, j + 2 * nj))],
        out_specs=[pl.BlockSpec((tm, tn), lambda i, j: (i, j))] * 2,
        out_shape=(jax.ShapeDtypeStruct((n, d), BF16), jax.ShapeDtypeStruct((n, d), F32)),
        scratch_shapes=[pltpu.VMEM((tm, d), BF16)],
        compiler_params=_params("arbitrary", "arbitrary"),
        name="conv_in",
    )(x, g, shift, scale, w3, w3, w3)


def _conv_out_kernel(p_ref, s1_ref, s2_ref, bg_ref, cw_ref, cb_ref, wo_ref, x_ref, gate_ref, o_ref, *, period):
    p = p_ref[...]
    t = lax.broadcasted_iota(jnp.int32, p.shape, 0) % period
    if period == p.shape[0]:
        s1 = jnp.broadcast_to(s1_ref[0, 0:1, :], p.shape)
        s2 = jnp.where(t == 0, jnp.broadcast_to(s2_ref[0, 0:1, :], p.shape),
                       jnp.broadcast_to(s2_ref[0, 1:2, :], p.shape))
    else:
        s1, s2 = s1_ref[...], s2_ref[...]
    p1 = jnp.where(t == 0, s1, pltpu.roll(p, 1, axis=0))
    p2 = jnp.where(t < 2, s2, pltpu.roll(p, 2, axis=0))
    y = cb_ref[...] + cw_ref[0:1, :] * p2
    y = y + cw_ref[1:2, :] * p1
    y = y + cw_ref[2:3, :] * p
    out = _dot((bg_ref[...].astype(F32) * y).astype(BF16), wo_ref[...])
    o_ref[...] = x_ref[...] + gate_ref[0] * out


def _conv_out(p, s1, s2, bg, cw, cb, wo, x, gate, *, blocks_per_group, period):
    n, d = x.shape
    tm = CONV_ROW_BLOCK
    row = lambda i: (i, 0)
    const2 = lambda i: (0, 0)
    if period == tm:
        s_spec = pl.BlockSpec((1, SUBLANES, d), lambda i: (i, 0, 0))
    else:
        s_spec = pl.BlockSpec((tm, d), row)
    return pl.pallas_call(
        functools.partial(_conv_out_kernel, period=period),
        grid=(n // tm,),
        in_specs=[pl.BlockSpec((tm, d), row), s_spec, s_spec, pl.BlockSpec((tm, d), row),
                  pl.BlockSpec(cw.shape, const2), pl.BlockSpec((1, d), const2),
                  pl.BlockSpec((d, d), const2), pl.BlockSpec((tm, d), row),
                  _mod_spec(gate, blocks_per_group)],
        out_specs=pl.BlockSpec((tm, d), row),
        out_shape=jax.ShapeDtypeStruct((n, d), F32),
        compiler_params=_params("arbitrary"),
        name="conv_out",
    )(p, s1, s2, bg, cw, cb, wo, x, gate)


def _batcher_pairs(n):
    pairs = []
    p = 1
    while p < n:
        k = p
        while k >= 1:
            for j in range(k % p, n - k, 2 * k):
                for i in range(min(k, n - j - k)):
                    if (i + j) // (2 * p) == (i + j + k) // (2 * p):
                        pairs.append((i + j, i + j + k))
            k //= 2
        p *= 2
    return pairs


def _exchange(v, i, j):
    v[i], v[j] = jnp.maximum(v[i], v[j]), jnp.minimum(v[i], v[j])


def _merge_top(a, b):
    n = len(a)
    c = [jnp.maximum(a[k], b[n - 1 - k]) for k in range(n)]
    d = n // 2
    while d >= 1:
        for i in range(n):
            if i & d == 0:
                _exchange(c, i, i + d)
        d //= 2
    return c


def _merge_sublanes(v):
    sh = SUBLANES // 2
    while sh >= 1:
        v = _merge_top(v, [pltpu.roll(x, sh, axis=0) for x in v])
        sh //= 2
    return v


def _top_sorted(s_ref):
    v = [s_ref[k * SUBLANES:(k + 1) * SUBLANES, :] for k in range(N_KEYS // SUBLANES)]
    for i, j in _batcher_pairs(len(v)):
        _exchange(v, i, j)
    return _merge_sublanes(v)


def _route_kernel(x_ref, g_ref, sh_ref, sc_ref, wqt_ref, k1_ref, k2_ref,
                  xt_ref, e1_ref, e2_ref, th_ref, qt_scr, s1_scr, s2_scr):
    assert N_KEYS // SUBLANES == PEER_TOPK
    tm = x_ref.shape[0]
    ht = _modnorm(x_ref[...], g_ref[...], sh_ref[0], sc_ref[0]).T.astype(BF16)
    xt_ref[...] = ht
    qt_scr[...] = _dot(wqt_ref[...], ht)
    d_key = 2 * N_KEYS
    sub = lax.broadcasted_iota(jnp.int32, (SUBLANES, tm), 0)

    def pack_rows(vals):
        out = vals[0]
        for s in range(1, SUBLANES):
            out = jnp.where(sub == s, vals[s], out)
        return out

    def head_body(hd, _):
        base = pl.multiple_of(hd * d_key, d_key)
        s1_scr[...] = _dot(k1_ref[...], qt_scr[pl.ds(base, N_KEYS), :].astype(BF16))
        s2_scr[...] = _dot(k2_ref[...], qt_scr[pl.ds(base + N_KEYS, N_KEYS), :].astype(BF16))
        t1 = _top_sorted(s1_scr)
        t2 = _top_sorted(s2_scr)
        a_lo = pack_rows(t1[:SUBLANES])
        a_hi = pack_rows(t1[SUBLANES:])
        lo = [a_lo + t for t in t2]
        hi = [a_hi + t for t in t2]
        top = _merge_sublanes(_merge_top(lo, hi))
        g16 = top[PEER_TOPK - 1]
        g17 = jnp.full((SUBLANES, tm), -jnp.inf, F32)
        for c in lo + hi:
            g17 = jnp.maximum(g17, jnp.where(c < g16, c, -jnp.inf))
        sh = SUBLANES // 2
        while sh >= 1:
            g17 = jnp.maximum(g17, pltpu.roll(g17, sh, axis=0))
            sh //= 2
        z = jnp.zeros((SUBLANES, tm), F32)
        for t in top:
            z = z + jnp.exp(t - top[0])
        row = lambda a: a[0:1, :]
        tau = 0.5 * (row(g16) + row(g17))
        s1 = s1_scr[...]
        s2 = s2_scr[...]
        e1_ref[hd] = jnp.where(s1 >= row(t1[PEER_TOPK - 1]), jnp.exp(s1 - row(t1[0])) / row(z), 0.0)
        e2_ref[hd] = jnp.where(s2 >= row(t2[PEER_TOPK - 1]), jnp.exp(s2 - row(t2[0])), 0.0)
        th_ref[hd] = jnp.exp((tau - row(t2[0])) - s1)
        return 0

    lax.fori_loop(0, PEER_HEADS, head_body, 0)


def _route(x, g, shift, scale, wqt, k1, k2, *, blocks_per_group):
    n, d = x.shape
    tm = ROUTE_BLOCK
    qw = wqt.shape[0]
    row = lambda i: (i, 0)
    const2 = lambda i: (0, 0)
    fac = jax.ShapeDtypeStruct((PEER_HEADS, N_KEYS, n), F32)
    fac_spec = pl.BlockSpec((PEER_HEADS, N_KEYS, tm), lambda i: (0, 0, i))
    key_scr = pltpu.VMEM((N_KEYS, tm), F32)
    return pl.pallas_call(
        _route_kernel,
        grid=(n // tm,),
        in_specs=[pl.BlockSpec((tm, d), row), pl.BlockSpec((1, d), const2),
                  _mod_spec(shift, blocks_per_group * (ROW_BLOCK // tm)),
                  _mod_spec(scale, blocks_per_group * (ROW_BLOCK // tm)),
                  pl.BlockSpec((qw, d), const2),
                  pl.BlockSpec((N_KEYS, N_KEYS), const2), pl.BlockSpec((N_KEYS, N_KEYS), const2)],
        out_specs=[pl.BlockSpec((d, tm), lambda i: (0, i)), fac_spec, fac_spec, fac_spec],
        out_shape=(jax.ShapeDtypeStruct((d, n), BF16), fac, fac, fac),
        scratch_shapes=[pltpu.VMEM((qw, tm), F32), key_scr, key_scr],
        compiler_params=_params("arbitrary"),
        name="peer_route",
    )(x, g, shift, scale, wqt, k1, k2)


def _peer_gate_piece(a_scr, h_scr, e1_ref, e2_ref, th_ref, chunk, ii, cols, key_rows):
    te = a_scr.shape[0]
    if ii not in key_rows:
        i_glob = chunk * (te // N_KEYS) + ii
        key_rows[ii] = [(th_ref[hd, pl.ds(i_glob, 1), :], e1_ref[hd, pl.ds(i_glob, 1), :])
                        for hd in range(PEER_HEADS)]
    gate = None
    for hd in range(PEER_HEADS):
        e2 = e2_ref[hd, :, cols]
        th, e1 = key_rows[ii][hd]
        term = jnp.where(e2 >= th[:, cols], e2, 0.0) * e1[:, cols]
        gate = term if gate is None else gate + term
    rows = slice(ii * N_KEYS, (ii + 1) * N_KEYS)
    h_scr[rows, cols] = (_gelu(a_scr[rows, cols]) * gate).astype(BF16)


def _peer_stage(xt_ref, u_ref, v_ref, yt_ref, a_in, a_out, h_in, h_out, facs, chunk):
    te, tm = a_in.shape
    d = yt_ref.shape[0]
    half = tm // 2
    col_halves = (slice(0, half), slice(half, tm))
    col_groups = [slice(k * LANES, (k + 1) * LANES) for k in range(tm // LANES)]
    gate_pieces = iter([(ii, cols) for ii in range(te // N_KEYS) for cols in col_groups])
    pre_pieces = [(slice(mi * 256, (mi + 1) * 256), cols) for mi in range(te // 256) for cols in col_halves]
    n_con = 2 * len(pre_pieces)
    con_rows = d // n_con
    con_pieces = [slice(k * con_rows, (k + 1) * con_rows) for k in range(n_con)]
    per_pre = (te // N_KEYS) * len(col_groups) // len(pre_pieces)
    key_rows = {}
    for k, (rows, cols) in enumerate(pre_pieces):
        a_out[rows, cols] = _dot(u_ref[rows, :], xt_ref[:, cols])
        for _ in range(per_pre // 2):
            _peer_gate_piece(a_in, h_out, *facs, chunk, *next(gate_pieces), key_rows)
        for r in con_pieces[2 * k:2 * k + 2]:
            yt_ref[r, :] += _dot(v_ref[r, :], h_in[...])
        for _ in range(per_pre - per_pre // 2):
            _peer_gate_piece(a_in, h_out, *facs, chunk, *next(gate_pieces), key_rows)


def _peer_kernel(xt_ref, u0_ref, ub_ref, un_ref, vp_ref, va_ref, e1_ref, e2_ref, th_ref, yt_ref,
                 a0_scr, a1_scr, h0_scr, h1_scr, *, n_chunks):
    c = pl.program_id(1)
    last = n_chunks - 1
    facs = (e1_ref, e2_ref, th_ref)

    @pl.when(c == 0)
    def _():
        yt_ref[...] = jnp.zeros(yt_ref.shape, F32)
        a0_scr[...] = _dot(u0_ref[...], xt_ref[...])
        h1_scr[...] = jnp.zeros(h1_scr.shape, BF16)

    _peer_stage(xt_ref, ub_ref, vp_ref, yt_ref, a0_scr, a1_scr, h1_scr, h0_scr, facs, jnp.minimum(2 * c, last))

    @pl.when(c < n_chunks // 2)
    def _():
        _peer_stage(xt_ref, un_ref, va_ref, yt_ref, a1_scr, a0_scr, h0_scr, h1_scr, facs, 2 * c + 1)


def _peer(xt, u, vt, e1, e2, th):
    d, n = xt.shape
    ne = u.shape[0]
    tm, te = ROW_BLOCK, EXPERT_BLOCK
    n_chunks = ne // te
    assert n_chunks % 2 == 0
    last = n_chunks - 1
    fac_spec = pl.BlockSpec((PEER_HEADS, N_KEYS, tm), lambda i, c: (0, 0, i))
    u_spec = lambda f: pl.BlockSpec((te, d), lambda i, c: (f(c), 0))
    v_spec = lambda f: pl.BlockSpec((d, te), lambda i, c: (0, f(c)))
    return pl.pallas_call(
        functools.partial(_peer_kernel, n_chunks=n_chunks),
        grid=(n // tm, n_chunks // 2 + 1),
        in_specs=[pl.BlockSpec((d, tm), lambda i, c: (0, i)),
                  u_spec(lambda c: 0),
                  u_spec(lambda c: jnp.minimum(2 * c + 1, last)),
                  u_spec(lambda c: jnp.minimum(2 * c + 2, last)),
                  v_spec(lambda c: jnp.maximum(2 * c - 1, 0)),
                  v_spec(lambda c: jnp.minimum(2 * c, last)),
                  fac_spec, fac_spec, fac_spec],
        out_specs=pl.BlockSpec((d, tm), lambda i, c: (0, i)),
        out_shape=jax.ShapeDtypeStruct((d, n), F32),
        scratch_shapes=[pltpu.VMEM((te, tm), F32), pltpu.VMEM((te, tm), F32),
                        pltpu.VMEM((te, tm), BF16), pltpu.VMEM((te, tm), BF16)],
        compiler_params=_params("arbitrary", "arbitrary", flags=PEER_SCHED_FLAGS),
        name="peer_dense",
    )(xt, u, u, u, vt, vt, e1, e2, th)


def _resid_kernel(x_ref, yt_ref, gate_ref, o_ref):
    o_ref[...] = x_ref[...] + gate_ref[0] * yt_ref[...].T


def _resid(x, yt, gate, *, blocks_per_group):
    n, d = x.shape
    tm = ROW_BLOCK
    row = lambda i: (i, 0)
    return pl.pallas_call(
        _resid_kernel,
        grid=(n // tm,),
        in_specs=[pl.BlockSpec((tm, d), row), pl.BlockSpec((d, tm), lambda i: (0, i)),
                  _mod_spec(gate, blocks_per_group)],
        out_specs=pl.BlockSpec((tm, d), row),
        out_shape=jax.ShapeDtypeStruct((n, d), F32),
        compiler_params=_params("arbitrary"),
        name="peer_resid",
    )(x, yt, gate)


class _Stream:
    def __init__(self, x, seq):
        self.batch = x.shape[0]
        self.seq = seq
        self.x = x.reshape(-1, x.shape[-1])
        self.long = seq % ROW_BLOCK == 0
        self.blocks_per_group = seq // ROW_BLOCK if self.long else 1

    def mod(self, m, k, tm=ROW_BLOCK):
        mk = m[:, k, :]
        if self.long:
            return mk[:, None, :]
        d = mk.shape[-1]
        return jnp.repeat(mk, self.seq, axis=0).reshape(-1, tm, d)


def kernel(x_prompt, x_sample, c_prompt, c_sample, cache_k, cache_v, cache_logf, page_table, state_conv,
           w_ada, b_ada, norm_mix, norm_ffn, w_in_ag, b_fgate, q_norm, k_norm, w_spatial, b_spatial,
           w_out_ag, w_in_conv, conv_w, conv_b, w_out_conv, peer_wq, peer_keys, peer_u, peer_v):
    bp, tp, d = x_prompt.shape
    bs, ts, _ = x_sample.shape
    depth = w_ada.shape[0]
    att_w = d // 2
    n_heads = att_w // HEAD_DIM
    n_groups = att_w // LANES
    assert tp % ROW_BLOCK == 0 and (bs * ts) % ROW_BLOCK == 0 and ROW_BLOCK % ts == 0 and ts == SUBLANES

    streams = [_Stream(x_prompt, tp), _Stream(x_sample, ts)]
    mod_all = _ada(jnp.concatenate([c_prompt, c_sample], axis=0), w_ada, b_ada)
    ck, cv = cache_k, cache_v
    clt = jnp.swapaxes(cache_logf, 2, 3)

    outs = {"k": [[], []], "v": [[], []], "f": [[], []], "g": [], "c": [[], []]}
    for layer in range(depth):
        ml = mod_all[layer].reshape(bp + bs, N_MOD, d)
        mods = [ml[:bp], ml[bp:]]
        g_mix = norm_mix[layer][None, :]
        g_ffn = norm_ffn[layer][None, :]
        if layer % 2 == 0:
            a = layer // 2
            w_in = w_in_ag[a]
            w5 = jnp.concatenate([w_in[:, :3 * att_w], w_in[:, 3 * att_w + n_heads:]], axis=1).astype(BF16)
            wf = jnp.pad(w_in[:, 3 * att_w:3 * att_w + n_heads], ((0, 0), (0, LANES - n_heads))).astype(BF16)
            bf = jnp.pad(b_fgate[a], (0, LANES - n_heads))[None, :]
            wa = w_out_ag[a][:att_w].astype(BF16)
            wg = w_out_ag[a][att_w:].astype(BF16)
            tril = jnp.tril(w_spatial[a])
        else:
            ci = layer // 2
            w3 = w_in_conv[ci].astype(BF16)
            wo = w_out_conv[ci].astype(BF16)
        wqt = peer_wq[layer].T.astype(BF16)
        k1 = peer_keys[layer, 0].astype(BF16)
        k2 = peer_keys[layer, 1].astype(BF16)
        pu = peer_u[layer].astype(BF16)
        pvt = peer_v[layer].T.astype(BF16)

        for si, st in enumerate(streams):
            m = mods[si]
            bpg = st.blocks_per_group
            if layer % 2 == 0:
                q, k, v, u, gv, lf = _ag_in(st.x, g_mix, st.mod(m, 0), st.mod(m, 1), w5, wf, bf,
                                            q_norm[a][None, :], k_norm[a][None, :],
                                            blocks_per_group=bpg, q_dtype=BF16 if st.long else F32)
                if st.long:
                    f, ft = _cumsum(lf.reshape(st.batch, st.seq, LANES))
                    tk = min(ATTN_K_BLOCK, st.seq)
                    ft4 = ft[:, :n_heads, :].reshape(st.batch, n_heads, st.seq // tk, tk)
                    attn = _attn_prompt(q.reshape(st.batch, st.seq, att_w), k.reshape(st.batch, st.seq, att_w),
                                        v.reshape(st.batch, st.seq, att_w), f, ft4)
                    mix = tril.astype(BF16)
                    bias = jnp.tile(jnp.repeat(b_spatial[a].T, LANES, axis=1), (ROW_BLOCK // CHUNK, 1))
                else:
                    attn = _attn_sample(page_table, q.reshape(st.batch, st.seq, att_w),
                                        k.reshape(st.batch, st.seq, att_w), v.reshape(st.batch, st.seq, att_w),
                                        lf.reshape(st.batch, st.seq, LANES), ck, cv, clt, a)
                    eye = jnp.eye(ROW_BLOCK // st.seq, dtype=F32)
                    mix = jax.vmap(lambda t: jnp.kron(eye, t))(tril[:, :st.seq, :st.seq]).astype(BF16)
                    bias = jnp.tile(jnp.repeat(b_spatial[a].T[:st.seq], LANES, axis=1), (ROW_BLOCK // st.seq, 1))
                    outs["g"].append(gv.reshape(st.batch, st.seq, att_w))
                st.x = _ag_out(attn.reshape(-1, att_w), u, gv, mix, bias, wa, wg, st.x, st.mod(m, 2),
                               blocks_per_group=bpg)
                outs["k"][si].append(k.reshape(st.batch, st.seq, n_heads, HEAD_DIM))
                outs["v"][si].append(v.reshape(st.batch, st.seq, n_heads, HEAD_DIM))
                outs["f"][si].append(lf[:, :n_heads].reshape(st.batch, st.seq, n_heads))
            else:
                bg, p = _conv_in(st.x, g_mix, st.mod(m, 0), st.mod(m, 1), w3, blocks_per_group=bpg)
                p3 = p.reshape(st.batch, st.seq, d)
                tc = CONV_ROW_BLOCK
                if st.long:
                    cb = st.seq // tc
                    pb = p.reshape(st.batch, cb, tc, d)[:, :, tc - SUBLANES:, :]
                    prev = jnp.concatenate([jnp.zeros_like(pb[:, :1]), pb[:, :-1]], axis=1)
                    prev = prev.reshape(st.batch * cb, SUBLANES, d)
                    s2 = prev[:, SUBLANES - 2:, :]
                    s2 = jnp.pad(s2, ((0, 0), (0, SUBLANES - 2), (0, 0)))
                    s1 = jnp.pad(prev[:, SUBLANES - 1:, :], ((0, 0), (0, SUBLANES - 1), (0, 0)))
                    period = tc
                else:
                    cb = 1
                    buf = state_conv[ci]
                    s2 = jnp.pad(buf, ((0, 0), (0, st.seq - 2), (0, 0))).reshape(-1, d)
                    s1 = jnp.pad(buf[:, 1:], ((0, 0), (0, st.seq - 1), (0, 0))).reshape(-1, d)
                    period = st.seq
                st.x = _conv_out(p, s1, s2, bg, conv_w[ci], conv_b[ci][None, :], wo, st.x, st.mod(m, 2, tc),
                                 blocks_per_group=cb, period=period)
                outs["c"][si].append(p3[:, st.seq - 2:, :])
            xt, e1, e2, th = _route(st.x, g_ffn, st.mod(m, 3, ROUTE_BLOCK), st.mod(m, 4, ROUTE_BLOCK),
                                    wqt, k1, k2, blocks_per_group=bpg)
            yt = _peer(xt, pu, pvt, e1, e2, th)
            st.x = _resid(st.x, yt, st.mod(m, 5), blocks_per_group=bpg)

    sp, ss = streams
    return (sp.x.reshape(bp, tp, d), ss.x.reshape(bs, ts, d),
            jnp.stack(outs["k"][0]), jnp.stack(outs["v"][0]), jnp.stack(outs["f"][0]),
            jnp.stack(outs["k"][1]), jnp.stack(outs["v"][1]), jnp.stack(outs["f"][1]),
            jnp.stack(outs["g"]), jnp.stack(outs["c"][0]), jnp.stack(outs["c"][1]))
```

```python
import functools

import jax
import jax.numpy as jnp
from jax import lax
from jax.experimental import pallas as pl
from jax.experimental.pallas import tpu as pltpu

F32 = jnp.float32
BF16 = jnp.bfloat16

EPS = 1e-6
LANES = 128
SUBLANES = 8
HEAD_DIM = 128
N_MOD = 6
PEER_HEADS = 8
PEER_TOPK = 16
N_KEYS = 128
PAGE_SIZE = 128
CHUNK = 128
VMEM_LIMIT_BYTES = 56 * 1024 * 1024
NEG_BIG = -1e30
ROW_BLOCK = 512
ROUTE_BLOCK = 256
EXPERT_BLOCK = 1024
PAGES_PER_STEP = 8
ATTN_Q_BLOCK = 512
ATTN_K_BLOCK = 512
ADA_COL_BLOCK = 1024
CONV_COL_BLOCK = 512
CONV_ROW_BLOCK = 256


def _params(*sem):
    return pltpu.CompilerParams(dimension_semantics=sem, vmem_limit_bytes=VMEM_LIMIT_BYTES)


def _modnorm(x, g, shift, scale):
    y = x * lax.rsqrt(jnp.mean(x * x, axis=-1, keepdims=True) + EPS)
    return (y * g) * (1.0 + scale) + shift


def _gelu(a):
    return jax.nn.gelu(a)


def _log_sigmoid(x):
    return jnp.minimum(x, 0.0) - jnp.log1p(jnp.exp(-jnp.abs(x)))


def _split3(x):
    hi = x.astype(BF16)
    r = x - hi.astype(F32)
    mid = r.astype(BF16)
    lo = (r - mid.astype(F32)).astype(BF16)
    return hi, mid, lo


def _dot(a, b):
    return jnp.dot(a, b, preferred_element_type=F32)


def _dot_nt(a, b):
    return lax.dot_general(a, b, (((1,), (1,)), ((), ())), preferred_element_type=F32)


def _mod_spec(mod, blocks_per_group):
    _, r, d = mod.shape
    if r == 1:
        return pl.BlockSpec((1, 1, d), lambda i, *_: (i // blocks_per_group, 0, 0))
    return pl.BlockSpec((1, r, d), lambda i, *_: (i, 0, 0))


def _ada_kernel(c_ref, w_ref, b_ref, o_ref):
    c = c_ref[...]
    s = (c * jax.nn.sigmoid(c)).astype(BF16)
    o_ref[0] = _dot(s, w_ref[0].astype(BF16)) + b_ref[0]


def _ada(c_all, w_ada, b_ada):
    depth, d, n6 = w_ada.shape
    r = c_all.shape[0]
    tn = ADA_COL_BLOCK
    return pl.pallas_call(
        _ada_kernel,
        grid=(depth, n6 // tn),
        in_specs=[pl.BlockSpec((r, d), lambda l, j: (0, 0)),
                  pl.BlockSpec((1, d, tn), lambda l, j: (l, 0, j)),
                  pl.BlockSpec((1, 1, tn), lambda l, j: (l, 0, j))],
        out_specs=pl.BlockSpec((1, r, tn), lambda l, j: (l, 0, j)),
        out_shape=jax.ShapeDtypeStruct((depth, r, n6), F32),
        compiler_params=_params("arbitrary", "arbitrary"),
        name="ada_mod",
    )(c_all, w_ada, b_ada.reshape(depth, 1, n6))


def _ag_in_kernel(x_ref, g_ref, sh_ref, sc_ref, w_ref, wf_ref, bf_ref, qg_ref, kg_ref,
                  q_ref, k_ref, v_ref, u_ref, gv_ref, lf_ref, h_scr, *, n_heads):
    j = pl.program_id(1)

    @pl.when(j == 0)
    def _():
        h = _modnorm(x_ref[...], g_ref[...], sh_ref[0], sc_ref[0]).astype(BF16)
        h_scr[...] = h
        lf_ref[...] = _log_sigmoid(_dot(h, wf_ref[...]) + bf_ref[...])

    def z():
        return _dot(h_scr[...], w_ref[...])

    def head_norm(zz, g, o_ref, post):
        for hd in range(n_heads):
            zh = zz[:, hd * HEAD_DIM:(hd + 1) * HEAD_DIM]
            y = zh * lax.rsqrt(jnp.mean(zh * zh, axis=-1, keepdims=True) + EPS) * g
            o_ref[:, hd * HEAD_DIM:(hd + 1) * HEAD_DIM] = (y * post).astype(o_ref.dtype)

    @pl.when(j == 0)
    def _():
        head_norm(z(), qg_ref[...], q_ref, HEAD_DIM ** -0.5)

    @pl.when(j == 1)
    def _():
        head_norm(z(), kg_ref[...], k_ref, 1.0)

    @pl.when(j == 2)
    def _():
        v_ref[...] = z()

    @pl.when(j == 3)
    def _():
        u_ref[...] = _gelu(z()).astype(u_ref.dtype)

    @pl.when(j == 4)
    def _():
        gv_ref[...] = _gelu(z())


def _ag_in(x, g, shift, scale, w5, wf, bf, qg, kg, *, blocks_per_group, q_dtype):
    n, d = x.shape
    w = w5.shape[1] // 5
    tm = ROW_BLOCK
    row = lambda i, j: (i, 0)
    const = lambda i, j: (0, 0)
    out_shapes = (jax.ShapeDtypeStruct((n, w), q_dtype),
                  jax.ShapeDtypeStruct((n, w), F32),
                  jax.ShapeDtypeStruct((n, w), F32),
                  jax.ShapeDtypeStruct((n, w), BF16),
                  jax.ShapeDtypeStruct((n, w), F32),
                  jax.ShapeDtypeStruct((n, LANES), F32))
    return pl.pallas_call(
        functools.partial(_ag_in_kernel, n_heads=w // HEAD_DIM),
        grid=(n // tm, 5),
        in_specs=[pl.BlockSpec((tm, d), row),
                  pl.BlockSpec((1, d), const),
                  _mod_spec(shift, blocks_per_group),
                  _mod_spec(scale, blocks_per_group),
                  pl.BlockSpec((d, w), lambda i, j: (0, j)),
                  pl.BlockSpec((d, LANES), const),
                  pl.BlockSpec((1, LANES), const),
                  pl.BlockSpec((1, HEAD_DIM), const),
                  pl.BlockSpec((1, HEAD_DIM), const)],
        out_specs=[pl.BlockSpec((tm, w), row)] * 5 + [pl.BlockSpec((tm, LANES), row)],
        out_shape=out_shapes,
        scratch_shapes=[pltpu.VMEM((tm, d), BF16)],
        compiler_params=_params("arbitrary", "arbitrary"),
        name="ag_in",
    )(x, g, shift, scale, w5, wf, bf, qg, kg)


def _tri_ones(n, strict):
    r = lax.broadcasted_iota(jnp.int32, (n, n), 0)
    c = lax.broadcasted_iota(jnp.int32, (n, n), 1)
    return jnp.where((r > c) if strict else (r >= c), 1.0, 0.0).astype(BF16)


def _cumsum_kernel(lf_ref, f_ref, ft_ref):
    t = lf_ref.shape[1]
    tri = _tri_ones(CHUNK, strict=False)
    carry = jnp.zeros((1, LANES), F32)
    for c in range(t // CHUNK):
        hi, mid, lo = _split3(lf_ref[0, c * CHUNK:(c + 1) * CHUNK, :])
        cs = (_dot(tri, hi) + _dot(tri, mid)) + _dot(tri, lo) + carry
        f_ref[0, c * CHUNK:(c + 1) * CHUNK, :] = cs
        carry = cs[CHUNK - 1:CHUNK, :]
    ft_ref[0] = f_ref[0].T


def _cumsum(lf):
    b, t, _ = lf.shape
    return pl.pallas_call(
        _cumsum_kernel,
        grid=(b,),
        in_specs=[pl.BlockSpec((1, t, LANES), lambda i: (i, 0, 0))],
        out_specs=[pl.BlockSpec((1, t, LANES), lambda i: (i, 0, 0)),
                   pl.BlockSpec((1, LANES, t), lambda i: (i, 0, 0))],
        out_shape=(jax.ShapeDtypeStruct((b, t, LANES), F32), jax.ShapeDtypeStruct((b, LANES, t), F32)),
        compiler_params=_params("arbitrary"),
        name="fox_cumsum",
    )(lf)


def _attn_kernel(q_ref, k_ref, v_ref, f_ref, ft_ref, o_ref, *, tq, tk):
    hd = pl.program_id(1)
    i = pl.program_id(2)
    q = q_ref[0]
    lane = lax.broadcasted_iota(jnp.int32, (tq, LANES), 1)
    fq = jnp.sum(jnp.where(lane == hd, f_ref[0], 0.0), axis=-1, keepdims=True)
    row = i * tq + lax.broadcasted_iota(jnp.int32, (tq, tk), 0)
    col = lax.broadcasted_iota(jnp.int32, (tq, tk), 1)
    n_kb = ((i + 1) * tq + tk - 1) // tk

    def body(j, carry):
        m, l, acc = carry
        start = pl.multiple_of(j * tk, tk)
        ks = k_ref[0, pl.ds(start, tk), :].astype(BF16)
        vs = v_ref[0, pl.ds(start, tk), :].astype(BF16)
        fk = ft_ref[0, hd, pl.ds(j, 1), :]
        s = _dot_nt(q, ks) + fq - fk
        s = jnp.where(col + j * tk <= row, s, NEG_BIG)
        m_new = jnp.maximum(m, jnp.max(s, axis=-1, keepdims=True))
        alpha = jnp.exp(m - m_new)
        p = jnp.exp(s - m_new)
        l = alpha * l + jnp.sum(p, axis=-1, keepdims=True)
        acc = alpha * acc + _dot(p.astype(BF16), vs)
        return m_new, l, acc

    m0 = jnp.full((tq, 1), NEG_BIG, F32)
    l0 = jnp.zeros((tq, 1), F32)
    a0 = jnp.zeros((tq, HEAD_DIM), F32)
    _, l, acc = lax.fori_loop(0, n_kb, body, (m0, l0, a0))
    o_ref[0] = (acc / l).astype(o_ref.dtype)


def _attn_prompt(q, k, v, f, ft4):
    b, t, w = k.shape
    n_heads = w // HEAD_DIM
    tq, tk = min(ATTN_Q_BLOCK, t), min(ATTN_K_BLOCK, t)
    return pl.pallas_call(
        functools.partial(_attn_kernel, tq=tq, tk=tk),
        grid=(b, n_heads, t // tq),
        in_specs=[pl.BlockSpec((1, tq, HEAD_DIM), lambda bi, h, i: (bi, i, h)),
                  pl.BlockSpec((1, t, HEAD_DIM), lambda bi, h, i: (bi, 0, h)),
                  pl.BlockSpec((1, t, HEAD_DIM), lambda bi, h, i: (bi, 0, h)),
                  pl.BlockSpec((1, tq, LANES), lambda bi, h, i: (bi, i, 0)),
                  pl.BlockSpec((1, n_heads, t // tk, tk), lambda bi, h, i: (bi, 0, 0, 0))],
        out_specs=pl.BlockSpec((1, tq, HEAD_DIM), lambda bi, h, i: (bi, i, h)),
        out_shape=jax.ShapeDtypeStruct((b, t, w), BF16),
        compiler_params=_params("arbitrary", "arbitrary", "arbitrary"),
        name="fox_prompt",
    )(q, k, v, f, ft4)


def _attn_sample_kernel(pt_ref, q_ref, kn_ref, vn_ref, lfn_ref, suall_ref, cu_ref, *rest, n_heads, pages_per_step):
    npg = pages_per_step
    kp_refs = rest[:npg]
    vp_refs = rest[npg:2 * npg]
    lp_refs = rest[2 * npg:3 * npg]
    o_ref = rest[3 * npg]
    m_scr, l_scr, acc_scr, carry_scr, colb_scr, cflat_scr = rest[3 * npg + 1:]
    g = pl.program_id(1)
    n_g = pl.num_programs(1)
    nr = q_ref.shape[1]
    pk = PAGE_SIZE * n_heads
    q = q_ref[0].astype(BF16)

    def head_match(cols):
        r = lax.broadcasted_iota(jnp.int32, (nr, cols), 0)
        c = lax.broadcasted_iota(jnp.int32, (nr, cols), 1)
        return r, c, (c % n_heads) == (r % n_heads)

    def online(logits, values):
        m_old = m_scr[...]
        m_new = m_old
        for s in logits:
            m_new = jnp.maximum(m_new, jnp.max(s, axis=-1, keepdims=True))
        alpha = jnp.exp(m_old - m_new)
        l_new = alpha * l_scr[...]
        acc = alpha * acc_scr[...]
        for s, vb in zip(logits, values):
            p = jnp.exp(s - m_new)
            l_new = l_new + jnp.sum(p, axis=-1, keepdims=True)
            acc = acc + _dot(p.astype(BF16), vb)
        l_scr[...] = l_new
        acc_scr[...] = acc
        m_scr[...] = m_new

    def stack3(x, rows):
        parts = [p.astype(F32) for p in _split3(x)]
        pad = rows - 3 * x.shape[0]
        return jnp.concatenate(parts + [jnp.zeros((pad, x.shape[1]), F32)], axis=0).astype(BF16)

    @pl.when(g == 0)
    def _():
        m_scr[...] = jnp.full(m_scr.shape, NEG_BIG, F32)
        l_scr[...] = jnp.zeros(l_scr.shape, F32)
        acc_scr[...] = jnp.zeros(acc_scr.shape, F32)
        carry_scr[...] = jnp.zeros(carry_scr.shape, F32)
        cs = _dot(stack3(lfn_ref[0], 2 * SUBLANES), cu_ref[...])
        cflat = cs[0:1, :] + cs[1:2, :] + cs[2:3, :]
        cflat_scr[...] = cflat
        r, c, _ = head_match(LANES)
        colb_scr[...] = jnp.sum(jnp.where(r == c, cflat, 0.0), axis=-1, keepdims=True)

    lf_rows = jnp.concatenate([lp_refs[r][0, 0] for r in range(npg)], axis=0)
    sums = _dot(stack3(lf_rows, 4 * npg), suall_ref[...])
    sums = sums[0:npg, :] + sums[npg:2 * npg, :] + sums[2 * npg:3 * npg, :]
    colb = colb_scr[...]
    _, _, hm = head_match(pk)
    carry = carry_scr[...]
    logits, values = [], []
    for r in range(npg):
        bias = sums[r:r + 1, 0:pk] + carry
        carry = carry + sums[r:r + 1, pk:2 * pk]
        kb = kp_refs[r][0, 0].reshape(pk, HEAD_DIM).astype(BF16)
        logits.append(jnp.where(hm, _dot_nt(q, kb) + bias + colb, NEG_BIG))
        values.append(vp_refs[r][0, 0].reshape(pk, HEAD_DIM).astype(BF16))
    online(logits, values)
    carry_scr[...] = carry

    @pl.when(g == n_g - 1)
    def _():
        zeros = jnp.zeros((LANES - nr, HEAD_DIM), F32)
        kb = jnp.concatenate([kn_ref[0], zeros], axis=0).astype(BF16)
        vb = jnp.concatenate([vn_ref[0], zeros], axis=0).astype(BF16)
        r, c, hmn = head_match(LANES)
        ok = hmn & (c // n_heads <= r // n_heads) & (c < nr)
        online([jnp.where(ok, _dot_nt(q, kb) + colb - cflat_scr[...], NEG_BIG)], [vb])
        o_ref[0] = (acc_scr[...] / l_scr[...]).astype(o_ref.dtype)


def _attn_sample(page_table, q, kn, vn, lfn, cache_k, cache_v, cache_lf, layer):
    nb, nr, _ = q.shape
    n_heads = cache_k.shape[3]
    pk = PAGE_SIZE * n_heads
    n_pages = page_table.shape[1]
    npg = PAGES_PER_STEP
    while n_pages % npg:
        npg //= 2
    assert nr <= LANES and npg % 4 == 0

    idx = jnp.arange(pk)
    same = (idx[:, None] % n_heads) == (idx[None, :] % n_heads)
    later = (idx[:, None] // n_heads) > (idx[None, :] // n_heads)
    suall = jnp.concatenate([same & later, same], axis=1).astype(BF16)
    il = jnp.arange(LANES)
    cu = ((il[:, None] % n_heads == il[None, :] % n_heads) & (il[:, None] // n_heads <= il[None, :] // n_heads)
          & (il[:, None] < nr) & (il[None, :] < nr)).astype(BF16)

    def page_spec(r, last):
        def imap(b, g, pt):
            return (layer, pt[b, n_pages - 1 - (g * npg + r)]) + (0,) * len(last)
        return pl.BlockSpec((1, 1) + last, imap)

    new_spec = lambda rows, last: pl.BlockSpec((1, rows, last), lambda b, g, pt: (b, 0, 0))
    const = lambda shape: pl.BlockSpec(shape, lambda b, g, pt: (0, 0))
    in_specs = ([new_spec(nr, HEAD_DIM)] * 3 + [new_spec(1, LANES), const(suall.shape), const(cu.shape)]
                + [page_spec(r, (PAGE_SIZE, n_heads, HEAD_DIM)) for r in range(npg)]
                + [page_spec(r, (PAGE_SIZE, n_heads, HEAD_DIM)) for r in range(npg)]
                + [page_spec(r, (1, pk)) for r in range(npg)])
    grid_spec = pltpu.PrefetchScalarGridSpec(
        num_scalar_prefetch=1,
        grid=(nb, n_pages // npg),
        in_specs=in_specs,
        out_specs=pl.BlockSpec((1, nr, HEAD_DIM), lambda b, g, pt: (b, 0, 0)),
        scratch_shapes=[pltpu.VMEM((nr, 1), F32), pltpu.VMEM((nr, 1), F32), pltpu.VMEM((nr, HEAD_DIM), F32),
                        pltpu.VMEM((1, pk), F32), pltpu.VMEM((nr, 1), F32), pltpu.VMEM((1, LANES), F32)])
    return pl.pallas_call(
        functools.partial(_attn_sample_kernel, n_heads=n_heads, pages_per_step=npg),
        grid_spec=grid_spec,
        out_shape=jax.ShapeDtypeStruct((nb, nr, HEAD_DIM), BF16),
        compiler_params=_params("arbitrary", "arbitrary"),
        name="fox_sample",
    )(page_table, q, kn, vn, lfn, suall, cu, *([cache_k] * npg), *([cache_v] * npg), *([cache_lf] * npg))


def _ag_out_kernel(attn_ref, u_ref, gv_ref, mix_ref, bias_ref, wa_ref, wg_ref, x_ref, gate_ref,
                   o_ref, gm_scr, *, n_groups, chunk):
    tm = x_ref.shape[0]
    for g in range(n_groups):
        cols = slice(g * LANES, (g + 1) * LANES)
        for c in range(tm // chunk):
            rows = slice(c * chunk, (c + 1) * chunk)
            s = _dot(mix_ref[g], gv_ref[rows, cols].astype(BF16)) + bias_ref[rows, cols]
            gm_scr[rows, cols] = (u_ref[rows, cols].astype(F32) * s).astype(BF16)
    out = _dot(attn_ref[...], wa_ref[...]) + _dot(gm_scr[...], wg_ref[...])
    o_ref[...] = x_ref[...] + gate_ref[0] * out


def _ag_out(attn, u, gv, mix, bias, wa, wg, x, gate, *, blocks_per_group):
    n, d = x.shape
    w = attn.shape[1]
    tm = ROW_BLOCK
    n_groups, chunk, _ = mix.shape
    row = lambda i: (i, 0)
    const2 = lambda i: (0, 0)
    return pl.pallas_call(
        functools.partial(_ag_out_kernel, n_groups=n_groups, chunk=chunk),
        grid=(n // tm,),
        in_specs=[pl.BlockSpec((tm, w), row), pl.BlockSpec((tm, w), row), pl.BlockSpec((tm, w), row),
                  pl.BlockSpec(mix.shape, lambda i: (0, 0, 0)),
                  pl.BlockSpec((tm, w), const2),
                  pl.BlockSpec((w, d), const2), pl.BlockSpec((w, d), const2),
                  pl.BlockSpec((tm, d), row),
                  _mod_spec(gate, blocks_per_group)],
        out_specs=pl.BlockSpec((tm, d), row),
        out_shape=jax.ShapeDtypeStruct((n, d), F32),
        scratch_shapes=[pltpu.VMEM((tm, w), BF16)],
        compiler_params=_params("arbitrary"),
        name="ag_out",
    )(attn, u, gv, mix, bias, wa, wg, x, gate)


def _conv_in_kernel(x_ref, g_ref, sh_ref, sc_ref, wb_ref, wc_ref, wz_ref, bg_ref, p_ref, h_scr):
    @pl.when(pl.program_id(1) == 0)
    def _():
        h_scr[...] = _modnorm(x_ref[...], g_ref[...], sh_ref[0], sc_ref[0]).astype(BF16)

    h = h_scr[...]
    bg_ref[...] = _dot(h, wb_ref[...]).astype(bg_ref.dtype)
    p_ref[...] = _dot(h, wc_ref[...]) * _dot(h, wz_ref[...])


def _conv_in(x, g, shift, scale, w3, *, blocks_per_group):
    n, d = x.shape
    tm, tn = ROW_BLOCK, CONV_COL_BLOCK
    nj = d // tn
    row = lambda i, j: (i, 0)
    return pl.pallas_call(
        _conv_in_kernel,
        grid=(n // tm, nj),
        in_specs=[pl.BlockSpec((tm, d), row),
                  pl.BlockSpec((1, d), lambda i, j: (0, 0)),
                  _mod_spec(shift, blocks_per_group),
                  _mod_spec(scale, blocks_per_group),
                  pl.BlockSpec((d, tn), lambda i, j: (0, j)),
                  pl.BlockSpec((d, tn), lambda i, j: (0, j + nj)),
                  pl.BlockSpec((d, tn), lambda i, j: (0, j + 2 * nj))],
        out_specs=[pl.BlockSpec((tm, tn), lambda i, j: (i, j))] * 2,
        out_shape=(jax.ShapeDtypeStruct((n, d), BF16), jax.ShapeDtypeStruct((n, d), F32)),
        scratch_shapes=[pltpu.VMEM((tm, d), BF16)],
        compiler_params=_params("arbitrary", "arbitrary"),
        name="conv_in",
    )(x, g, shift, scale, w3, w3, w3)


def _conv_out_kernel(p_ref, s1_ref, s2_ref, bg_ref, cw_ref, cb_ref, wo_ref, x_ref, gate_ref, o_ref, *, period):
    p = p_ref[...]
    t = lax.broadcasted_iota(jnp.int32, p.shape, 0) % period
    if period == p.shape[0]:
        s1 = jnp.broadcast_to(s1_ref[0, 0:1, :], p.shape)
        s2 = jnp.where(t == 0, jnp.broadcast_to(s2_ref[0, 0:1, :], p.shape),
                       jnp.broadcast_to(s2_ref[0, 1:2, :], p.shape))
    else:
        s1, s2 = s1_ref[...], s2_ref[...]
    p1 = jnp.where(t == 0, s1, pltpu.roll(p, 1, axis=0))
    p2 = jnp.where(t < 2, s2, pltpu.roll(p, 2, axis=0))
    y = cb_ref[...] + cw_ref[0:1, :] * p2
    y = y + cw_ref[1:2, :] * p1
    y = y + cw_ref[2:3, :] * p
    out = _dot((bg_ref[...].astype(F32) * y).astype(BF16), wo_ref[...])
    o_ref[...] = x_ref[...] + gate_ref[0] * out


def _conv_out(p, s1, s2, bg, cw, cb, wo, x, gate, *, blocks_per_group, period):
    n, d = x.shape
    tm = CONV_ROW_BLOCK
    row = lambda i: (i, 0)
    const2 = lambda i: (0, 0)
    if period == tm:
        s_spec = pl.BlockSpec((1, SUBLANES, d), lambda i: (i, 0, 0))
    else:
        s_spec = pl.BlockSpec((tm, d), row)
    return pl.pallas_call(
        functools.partial(_conv_out_kernel, period=period),
        grid=(n // tm,),
        in_specs=[pl.BlockSpec((tm, d), row), s_spec, s_spec, pl.BlockSpec((tm, d), row),
                  pl.BlockSpec(cw.shape, const2), pl.BlockSpec((1, d), const2),
                  pl.BlockSpec((d, d), const2), pl.BlockSpec((tm, d), row),
                  _mod_spec(gate, blocks_per_group)],
        out_specs=pl.BlockSpec((tm, d), row),
        out_shape=jax.ShapeDtypeStruct((n, d), F32),
        compiler_params=_params("arbitrary"),
        name="conv_out",
    )(p, s1, s2, bg, cw, cb, wo, x, gate)


def _batcher_pairs(n):
    pairs = []
    p = 1
    while p < n:
        k = p
        while k >= 1:
            for j in range(k % p, n - k, 2 * k):
                for i in range(min(k, n - j - k)):
                    if (i + j) // (2 * p) == (i + j + k) // (2 * p):
                        pairs.append((i + j, i + j + k))
            k //= 2
        p *= 2
    return pairs


def _exchange(v, i, j):
    v[i], v[j] = jnp.maximum(v[i], v[j]), jnp.minimum(v[i], v[j])


def _merge_top(a, b):
    n = len(a)
    c = [jnp.maximum(a[k], b[n - 1 - k]) for k in range(n)]
    d = n // 2
    while d >= 1:
        for i in range(n):
            if i & d == 0:
                _exchange(c, i, i + d)
        d //= 2
    return c


def _merge_sublanes(v):
    sh = SUBLANES // 2
    while sh >= 1:
        v = _merge_top(v, [pltpu.roll(x, sh, axis=0) for x in v])
        sh //= 2
    return v


def _top_sorted(s_ref):
    v = [s_ref[k * SUBLANES:(k + 1) * SUBLANES, :] for k in range(N_KEYS // SUBLANES)]
    for i, j in _batcher_pairs(len(v)):
        _exchange(v, i, j)
    return _merge_sublanes(v)


def _route_kernel(x_ref, g_ref, sh_ref, sc_ref, wqt_ref, k1_ref, k2_ref,
                  xt_ref, e1_ref, e2_ref, th_ref, qt_scr, s1_scr, s2_scr):
    assert N_KEYS // SUBLANES == PEER_TOPK
    tm = x_ref.shape[0]
    ht = _modnorm(x_ref[...], g_ref[...], sh_ref[0], sc_ref[0]).T.astype(BF16)
    xt_ref[...] = ht
    qt_scr[...] = _dot(wqt_ref[...], ht)
    d_key = 2 * N_KEYS
    sub = lax.broadcasted_iota(jnp.int32, (SUBLANES, tm), 0)

    def pack_rows(vals):
        out = vals[0]
        for s in range(1, SUBLANES):
            out = jnp.where(sub == s, vals[s], out)
        return out

    def head_body(hd, _):
        base = pl.multiple_of(hd * d_key, d_key)
        s1_scr[...] = _dot(k1_ref[...], qt_scr[pl.ds(base, N_KEYS), :].astype(BF16))
        s2_scr[...] = _dot(k2_ref[...], qt_scr[pl.ds(base + N_KEYS, N_KEYS), :].astype(BF16))
        t1 = _top_sorted(s1_scr)
        t2 = _top_sorted(s2_scr)
        a_lo = pack_rows(t1[:SUBLANES])
        a_hi = pack_rows(t1[SUBLANES:])
        lo = [a_lo + t for t in t2]
        hi = [a_hi + t for t in t2]
        top = _merge_sublanes(_merge_top(lo, hi))
        g16 = top[PEER_TOPK - 1]
        g17 = jnp.full((SUBLANES, tm), -jnp.inf, F32)
        for c in lo + hi:
            g17 = jnp.maximum(g17, jnp.where(c < g16, c, -jnp.inf))
        sh = SUBLANES // 2
        while sh >= 1:
            g17 = jnp.maximum(g17, pltpu.roll(g17, sh, axis=0))
            sh //= 2
        z = jnp.zeros((SUBLANES, tm), F32)
        for t in top:
            z = z + jnp.exp(t - top[0])
        row = lambda a: a[0:1, :]
        tau = 0.5 * (row(g16) + row(g17))
        s1 = s1_scr[...]
        s2 = s2_scr[...]
        e1_ref[hd] = jnp.where(s1 >= row(t1[PEER_TOPK - 1]), jnp.exp(s1 - row(t1[0])) / row(z), 0.0)
        e2_ref[hd] = jnp.where(s2 >= row(t2[PEER_TOPK - 1]), jnp.exp(s2 - row(t2[0])), 0.0)
        th_ref[hd] = jnp.exp((tau - row(t2[0])) - s1)
        return 0

    lax.fori_loop(0, PEER_HEADS, head_body, 0)


def _route(x, g, shift, scale, wqt, k1, k2, *, blocks_per_group):
    n, d = x.shape
    tm = ROUTE_BLOCK
    qw = wqt.shape[0]
    row = lambda i: (i, 0)
    const2 = lambda i: (0, 0)
    fac = jax.ShapeDtypeStruct((PEER_HEADS, N_KEYS, n), F32)
    fac_spec = pl.BlockSpec((PEER_HEADS, N_KEYS, tm), lambda i: (0, 0, i))
    key_scr = pltpu.VMEM((N_KEYS, tm), F32)
    return pl.pallas_call(
        _route_kernel,
        grid=(n // tm,),
        in_specs=[pl.BlockSpec((tm, d), row), pl.BlockSpec((1, d), const2),
                  _mod_spec(shift, blocks_per_group * (ROW_BLOCK // tm)),
                  _mod_spec(scale, blocks_per_group * (ROW_BLOCK // tm)),
                  pl.BlockSpec((qw, d), const2),
                  pl.BlockSpec((N_KEYS, N_KEYS), const2), pl.BlockSpec((N_KEYS, N_KEYS), const2)],
        out_specs=[pl.BlockSpec((d, tm), lambda i: (0, i)), fac_spec, fac_spec, fac_spec],
        out_shape=(jax.ShapeDtypeStruct((d, n), BF16), fac, fac, fac),
        scratch_shapes=[pltpu.VMEM((qw, tm), F32), key_scr, key_scr],
        compiler_params=_params("arbitrary"),
        name="peer_route",
    )(x, g, shift, scale, wqt, k1, k2)


def _peer_gate_piece(a_scr, h_scr, e1_ref, e2_ref, th_ref, chunk, ii, cols, key_rows):
    te = a_scr.shape[0]
    if ii not in key_rows:
        i_glob = chunk * (te // N_KEYS) + ii
        key_rows[ii] = [(th_ref[hd, pl.ds(i_glob, 1), :], e1_ref[hd, pl.ds(i_glob, 1), :])
                        for hd in range(PEER_HEADS)]
    gate = None
    for hd in range(PEER_HEADS):
        e2 = e2_ref[hd, :, cols]
        th, e1 = key_rows[ii][hd]
        term = jnp.where(e2 >= th[:, cols], e2, 0.0) * e1[:, cols]
        gate = term if gate is None else gate + term
    rows = slice(ii * N_KEYS, (ii + 1) * N_KEYS)
    h_scr[rows, cols] = (_gelu(a_scr[rows, cols]) * gate).astype(BF16)


def _peer_kernel(xt_ref, u_ref, vt_ref, e1_ref, e2_ref, th_ref, yt_ref, a_scr, h_scr):
    c = pl.program_id(1)
    te, tm = a_scr.shape

    @pl.when(c == 0)
    def _():
        yt_ref[...] = jnp.zeros(yt_ref.shape, F32)

    a_scr[...] = _dot(u_ref[...], xt_ref[...])
    key_rows = {}
    for ii in range(te // N_KEYS):
        for k in range(tm // LANES):
            _peer_gate_piece(a_scr, h_scr, e1_ref, e2_ref, th_ref, c, ii, slice(k * LANES, (k + 1) * LANES),
                             key_rows)
    yt_ref[...] += _dot(vt_ref[...], h_scr[...])


def _peer(xt, u, vt, e1, e2, th):
    d, n = xt.shape
    ne = u.shape[0]
    tm, te = ROW_BLOCK, EXPERT_BLOCK
    fac_spec = pl.BlockSpec((PEER_HEADS, N_KEYS, tm), lambda i, c: (0, 0, i))
    return pl.pallas_call(
        _peer_kernel,
        grid=(n // tm, ne // te),
        in_specs=[pl.BlockSpec((d, tm), lambda i, c: (0, i)),
                  pl.BlockSpec((te, d), lambda i, c: (c, 0)),
                  pl.BlockSpec((d, te), lambda i, c: (0, c)),
                  fac_spec, fac_spec, fac_spec],
        out_specs=pl.BlockSpec((d, tm), lambda i, c: (0, i)),
        out_shape=jax.ShapeDtypeStruct((d, n), F32),
        scratch_shapes=[pltpu.VMEM((te, tm), F32), pltpu.VMEM((te, tm), BF16)],
        compiler_params=_params("arbitrary", "arbitrary"),
        name="peer_dense",
    )(xt, u, vt, e1, e2, th)


def _resid_kernel(x_ref, yt_ref, gate_ref, o_ref):
    o_ref[...] = x_ref[...] + gate_ref[0] * yt_ref[...].T


def _resid(x, yt, gate, *, blocks_per_group):
    n, d = x.shape
    tm = ROW_BLOCK
    row = lambda i: (i, 0)
    return pl.pallas_call(
        _resid_kernel,
        grid=(n // tm,),
        in_specs=[pl.BlockSpec((tm, d), row), pl.BlockSpec((d, tm), lambda i: (0, i)),
                  _mod_spec(gate, blocks_per_group)],
        out_specs=pl.BlockSpec((tm, d), row),
        out_shape=jax.ShapeDtypeStruct((n, d), F32),
        compiler_params=_params("arbitrary"),
        name="peer_resid",
    )(x, yt, gate)


class _Stream:
    def __init__(self, x, seq):
        self.batch = x.shape[0]
        self.seq = seq
        self.x = x.reshape(-1, x.shape[-1])
        self.long = seq % ROW_BLOCK == 0
        self.blocks_per_group = seq // ROW_BLOCK if self.long else 1

    def mod(self, m, k, tm=ROW_BLOCK):
        mk = m[:, k, :]
        if self.long:
            return mk[:, None, :]
        d = mk.shape[-1]
        return jnp.repeat(mk, self.seq, axis=0).reshape(-1, tm, d)


def kernel(x_prompt, x_sample, c_prompt, c_sample, cache_k, cache_v, cache_logf, page_table, state_conv,
           w_ada, b_ada, norm_mix, norm_ffn, w_in_ag, b_fgate, q_norm, k_norm, w_spatial, b_spatial,
           w_out_ag, w_in_conv, conv_w, conv_b, w_out_conv, peer_wq, peer_keys, peer_u, peer_v):
    bp, tp, d = x_prompt.shape
    bs, ts, _ = x_sample.shape
    depth = w_ada.shape[0]
    att_w = d // 2
    n_heads = att_w // HEAD_DIM
    n_groups = att_w // LANES
    assert tp % ROW_BLOCK == 0 and (bs * ts) % ROW_BLOCK == 0 and ROW_BLOCK % ts == 0 and ts == SUBLANES

    streams = [_Stream(x_prompt, tp), _Stream(x_sample, ts)]
    mod_all = _ada(jnp.concatenate([c_prompt, c_sample], axis=0), w_ada, b_ada)
    ck, cv = cache_k, cache_v
    clf = cache_logf.reshape(cache_logf.shape[0], cache_logf.shape[1], 1, PAGE_SIZE * n_heads)

    outs = {"k": [[], []], "v": [[], []], "f": [[], []], "g": [], "c": [[], []]}
    for layer in range(depth):
        ml = mod_all[layer].reshape(bp + bs, N_MOD, d)
        mods = [ml[:bp], ml[bp:]]
        g_mix = norm_mix[layer][None, :]
        g_ffn = norm_ffn[layer][None, :]
        if layer % 2 == 0:
            a = layer // 2
            w_in = w_in_ag[a]
            w5 = jnp.concatenate([w_in[:, :3 * att_w], w_in[:, 3 * att_w + n_heads:]], axis=1).astype(BF16)
            wf = jnp.pad(w_in[:, 3 * att_w:3 * att_w + n_heads], ((0, 0), (0, LANES - n_heads))).astype(BF16)
            bf = jnp.pad(b_fgate[a], (0, LANES - n_heads))[None, :]
            wa = w_out_ag[a][:att_w].astype(BF16)
            wg = w_out_ag[a][att_w:].astype(BF16)
            tril = jnp.tril(w_spatial[a])
        else:
            ci = layer // 2
            w3 = w_in_conv[ci].astype(BF16)
            wo = w_out_conv[ci].astype(BF16)
        wqt = peer_wq[layer].T.astype(BF16)
        k1 = peer_keys[layer, 0].astype(BF16)
        k2 = peer_keys[layer, 1].astype(BF16)
        pu = peer_u[layer].astype(BF16)
        pvt = peer_v[layer].T.astype(BF16)

        for si, st in enumerate(streams):
            m = mods[si]
            bpg = st.blocks_per_group
            if layer % 2 == 0:
                q, k, v, u, gv, lf = _ag_in(st.x, g_mix, st.mod(m, 0), st.mod(m, 1), w5, wf, bf,
                                            q_norm[a][None, :], k_norm[a][None, :],
                                            blocks_per_group=bpg, q_dtype=BF16 if st.long else F32)
                if st.long:
                    f, ft = _cumsum(lf.reshape(st.batch, st.seq, LANES))
                    tk = min(ATTN_K_BLOCK, st.seq)
                    ft4 = ft[:, :n_heads, :].reshape(st.batch, n_heads, st.seq // tk, tk)
                    attn = _attn_prompt(q.reshape(st.batch, st.seq, att_w), k.reshape(st.batch, st.seq, att_w),
                                        v.reshape(st.batch, st.seq, att_w), f, ft4)
                    mix = tril.astype(BF16)
                    bias = jnp.tile(jnp.repeat(b_spatial[a].T, LANES, axis=1), (ROW_BLOCK // CHUNK, 1))
                else:
                    nr = st.seq * n_heads
                    lfn = jnp.pad(lf[:, :n_heads].reshape(st.batch, 1, nr), ((0, 0), (0, 0), (0, LANES - nr)))
                    attn = _attn_sample(page_table, q.reshape(st.batch, nr, HEAD_DIM),
                                        k.reshape(st.batch, nr, HEAD_DIM), v.reshape(st.batch, nr, HEAD_DIM),
                                        lfn, ck, cv, clf, a)
                    eye = jnp.eye(ROW_BLOCK // st.seq, dtype=F32)
                    mix = jax.vmap(lambda t: jnp.kron(eye, t))(tril[:, :st.seq, :st.seq]).astype(BF16)
                    bias = jnp.tile(jnp.repeat(b_spatial[a].T[:st.seq], LANES, axis=1), (ROW_BLOCK // st.seq, 1))
                    outs["g"].append(gv.reshape(st.batch, st.seq, att_w))
                st.x = _ag_out(attn.reshape(-1, att_w), u, gv, mix, bias, wa, wg, st.x, st.mod(m, 2),
                               blocks_per_group=bpg)
                outs["k"][si].append(k.reshape(st.batch, st.seq, n_heads, HEAD_DIM))
                outs["v"][si].append(v.reshape(st.batch, st.seq, n_heads, HEAD_DIM))
                outs["f"][si].append(lf[:, :n_heads].reshape(st.batch, st.seq, n_heads))
            else:
                bg, p = _conv_in(st.x, g_mix, st.mod(m, 0), st.mod(m, 1), w3, blocks_per_group=bpg)
                p3 = p.reshape(st.batch, st.seq, d)
                tc = CONV_ROW_BLOCK
                if st.long:
                    cb = st.seq // tc
                    pb = p.reshape(st.batch, cb, tc, d)[:, :, tc - SUBLANES:, :]
                    prev = jnp.concatenate([jnp.zeros_like(pb[:, :1]), pb[:, :-1]], axis=1)
                    prev = prev.reshape(st.batch * cb, SUBLANES, d)
                    s2 = prev[:, SUBLANES - 2:, :]
                    s2 = jnp.pad(s2, ((0, 0), (0, SUBLANES - 2), (0, 0)))
                    s1 = jnp.pad(prev[:, SUBLANES - 1:, :], ((0, 0), (0, SUBLANES - 1), (0, 0)))
                    period = tc
                else:
                    cb = 1
                    buf = state_conv[ci]
                    s2 = jnp.pad(buf, ((0, 0), (0, st.seq - 2), (0, 0))).reshape(-1, d)
                    s1 = jnp.pad(buf[:, 1:], ((0, 0), (0, st.seq - 1), (0, 0))).reshape(-1, d)
                    period = st.seq
                st.x = _conv_out(p, s1, s2, bg, conv_w[ci], conv_b[ci][None, :], wo, st.x, st.mod(m, 2, tc),
                                 blocks_per_group=cb, period=period)
                outs["c"][si].append(p3[:, st.seq - 2:, :])
            xt, e1, e2, th = _route(st.x, g_ffn, st.mod(m, 3, ROUTE_BLOCK), st.mod(m, 4, ROUTE_BLOCK),
                                    wqt, k1, k2, blocks_per_group=bpg)
            yt = _peer(xt, pu, pvt, e1, e2, th)
            st.x = _resid(st.x, yt, st.mod(m, 5), blocks_per_group=bpg)

    sp, ss = streams
    return (sp.x.reshape(bp, tp, d), ss.x.reshape(bs, ts, d),
            jnp.stack(outs["k"][0]), jnp.stack(outs["v"][0]), jnp.stack(outs["f"][0]),
            jnp.stack(outs["k"][1]), jnp.stack(outs["v"][1]), jnp.stack(outs["f"][1]),
            jnp.stack(outs["g"]), jnp.stack(outs["c"][0]), jnp.stack(outs["c"][1]))
```

```python
import functools

import jax
import jax.numpy as jnp
from jax import lax
from jax.experimental import pallas as pl
from jax.experimental.pallas import tpu as pltpu

F32 = jnp.float32
BF16 = jnp.bfloat16

EPS = 1e-6
LANES = 128
SUBLANES = 8
HEAD_DIM = 128
N_MOD = 6
PEER_HEADS = 8
PEER_TOPK = 16
N_KEYS = 128
PAGE_SIZE = 128
CHUNK = 128
VMEM_LIMIT_BYTES = 56 * 1024 * 1024
NEG_BIG = -1e30
ROW_BLOCK = 512
ROUTE_BLOCK = 256
EXPERT_BLOCK = 1024
PAGES_PER_STEP = 16
ATTN_Q_BLOCK = 512
ATTN_K_BLOCK = 512
ADA_COL_BLOCK = 1024
CONV_COL_BLOCK = 512
CONV_ROW_BLOCK = 256


def _params(*sem):
    return pltpu.CompilerParams(dimension_semantics=sem, vmem_limit_bytes=VMEM_LIMIT_BYTES)


def _modnorm(x, g, shift, scale):
    y = x * lax.rsqrt(jnp.mean(x * x, axis=-1, keepdims=True) + EPS)
    return (y * g) * (1.0 + scale) + shift


def _gelu(a):
    return jax.nn.gelu(a)


def _log_sigmoid(x):
    return jnp.minimum(x, 0.0) - jnp.log1p(jnp.exp(-jnp.abs(x)))


def _split3(x):
    hi = x.astype(BF16)
    r = x - hi.astype(F32)
    mid = r.astype(BF16)
    lo = (r - mid.astype(F32)).astype(BF16)
    return hi, mid, lo


def _dot(a, b):
    return jnp.dot(a, b, preferred_element_type=F32)


def _dot_nt(a, b):
    return lax.dot_general(a, b, (((1,), (1,)), ((), ())), preferred_element_type=F32)


def _mod_spec(mod, blocks_per_group):
    _, r, d = mod.shape
    if r == 1:
        return pl.BlockSpec((1, 1, d), lambda i, *_: (i // blocks_per_group, 0, 0))
    return pl.BlockSpec((1, r, d), lambda i, *_: (i, 0, 0))


def _ada_kernel(c_ref, w_ref, b_ref, o_ref):
    c = c_ref[...]
    s = (c * jax.nn.sigmoid(c)).astype(BF16)
    o_ref[0] = _dot(s, w_ref[0].astype(BF16)) + b_ref[0]


def _ada(c_all, w_ada, b_ada):
    depth, d, n6 = w_ada.shape
    r = c_all.shape[0]
    tn = ADA_COL_BLOCK
    return pl.pallas_call(
        _ada_kernel,
        grid=(depth, n6 // tn),
        in_specs=[pl.BlockSpec((r, d), lambda l, j: (0, 0)),
                  pl.BlockSpec((1, d, tn), lambda l, j: (l, 0, j)),
                  pl.BlockSpec((1, 1, tn), lambda l, j: (l, 0, j))],
        out_specs=pl.BlockSpec((1, r, tn), lambda l, j: (l, 0, j)),
        out_shape=jax.ShapeDtypeStruct((depth, r, n6), F32),
        compiler_params=_params("arbitrary", "arbitrary"),
        name="ada_mod",
    )(c_all, w_ada, b_ada.reshape(depth, 1, n6))


def _ag_in_kernel(x_ref, g_ref, sh_ref, sc_ref, w_ref, wf_ref, bf_ref, qg_ref, kg_ref,
                  q_ref, k_ref, v_ref, u_ref, gv_ref, lf_ref, h_scr, *, n_heads):
    j = pl.program_id(1)

    @pl.when(j == 0)
    def _():
        h = _modnorm(x_ref[...], g_ref[...], sh_ref[0], sc_ref[0]).astype(BF16)
        h_scr[...] = h
        lf_ref[...] = _log_sigmoid(_dot(h, wf_ref[...]) + bf_ref[...])

    def z():
        return _dot(h_scr[...], w_ref[...])

    def head_norm(zz, g, o_ref, post):
        for hd in range(n_heads):
            zh = zz[:, hd * HEAD_DIM:(hd + 1) * HEAD_DIM]
            y = zh * lax.rsqrt(jnp.mean(zh * zh, axis=-1, keepdims=True) + EPS) * g
            o_ref[:, hd * HEAD_DIM:(hd + 1) * HEAD_DIM] = (y * post).astype(o_ref.dtype)

    @pl.when(j == 0)
    def _():
        head_norm(z(), qg_ref[...], q_ref, HEAD_DIM ** -0.5)

    @pl.when(j == 1)
    def _():
        head_norm(z(), kg_ref[...], k_ref, 1.0)

    @pl.when(j == 2)
    def _():
        v_ref[...] = z()

    @pl.when(j == 3)
    def _():
        u_ref[...] = _gelu(z()).astype(u_ref.dtype)

    @pl.when(j == 4)
    def _():
        gv_ref[...] = _gelu(z())


def _ag_in(x, g, shift, scale, w5, wf, bf, qg, kg, *, blocks_per_group, q_dtype):
    n, d = x.shape
    w = w5.shape[1] // 5
    tm = ROW_BLOCK
    row = lambda i, j: (i, 0)
    const = lambda i, j: (0, 0)
    out_shapes = (jax.ShapeDtypeStruct((n, w), q_dtype),
                  jax.ShapeDtypeStruct((n, w), F32),
                  jax.ShapeDtypeStruct((n, w), F32),
                  jax.ShapeDtypeStruct((n, w), BF16),
                  jax.ShapeDtypeStruct((n, w), F32),
                  jax.ShapeDtypeStruct((n, LANES), F32))
    return pl.pallas_call(
        functools.partial(_ag_in_kernel, n_heads=w // HEAD_DIM),
        grid=(n // tm, 5),
        in_specs=[pl.BlockSpec((tm, d), row),
                  pl.BlockSpec((1, d), const),
                  _mod_spec(shift, blocks_per_group),
                  _mod_spec(scale, blocks_per_group),
                  pl.BlockSpec((d, w), lambda i, j: (0, j)),
                  pl.BlockSpec((d, LANES), const),
                  pl.BlockSpec((1, LANES), const),
                  pl.BlockSpec((1, HEAD_DIM), const),
                  pl.BlockSpec((1, HEAD_DIM), const)],
        out_specs=[pl.BlockSpec((tm, w), row)] * 5 + [pl.BlockSpec((tm, LANES), row)],
        out_shape=out_shapes,
        scratch_shapes=[pltpu.VMEM((tm, d), BF16)],
        compiler_params=_params("arbitrary", "arbitrary"),
        name="ag_in",
    )(x, g, shift, scale, w5, wf, bf, qg, kg)


def _tri_ones(n, strict):
    r = lax.broadcasted_iota(jnp.int32, (n, n), 0)
    c = lax.broadcasted_iota(jnp.int32, (n, n), 1)
    return jnp.where((r > c) if strict else (r >= c), 1.0, 0.0).astype(BF16)


def _cumsum_kernel(lf_ref, f_ref, ft_ref):
    t = lf_ref.shape[1]
    tri = _tri_ones(CHUNK, strict=False)
    carry = jnp.zeros((1, LANES), F32)
    for c in range(t // CHUNK):
        hi, mid, lo = _split3(lf_ref[0, c * CHUNK:(c + 1) * CHUNK, :])
        cs = (_dot(tri, hi) + _dot(tri, mid)) + _dot(tri, lo) + carry
        f_ref[0, c * CHUNK:(c + 1) * CHUNK, :] = cs
        carry = cs[CHUNK - 1:CHUNK, :]
    ft_ref[0] = f_ref[0].T


def _cumsum(lf):
    b, t, _ = lf.shape
    return pl.pallas_call(
        _cumsum_kernel,
        grid=(b,),
        in_specs=[pl.BlockSpec((1, t, LANES), lambda i: (i, 0, 0))],
        out_specs=[pl.BlockSpec((1, t, LANES), lambda i: (i, 0, 0)),
                   pl.BlockSpec((1, LANES, t), lambda i: (i, 0, 0))],
        out_shape=(jax.ShapeDtypeStruct((b, t, LANES), F32), jax.ShapeDtypeStruct((b, LANES, t), F32)),
        compiler_params=_params("arbitrary"),
        name="fox_cumsum",
    )(lf)


def _attn_kernel(q_ref, k_ref, v_ref, f_ref, ft_ref, o_ref, *, tq, tk):
    hd = pl.program_id(1)
    i = pl.program_id(2)
    q = q_ref[0]
    lane = lax.broadcasted_iota(jnp.int32, (tq, LANES), 1)
    fq = jnp.sum(jnp.where(lane == hd, f_ref[0], 0.0), axis=-1, keepdims=True)
    row = i * tq + lax.broadcasted_iota(jnp.int32, (tq, tk), 0)
    col = lax.broadcasted_iota(jnp.int32, (tq, tk), 1)
    n_kb = ((i + 1) * tq + tk - 1) // tk

    def body(j, carry, masked):
        m, l, acc = carry
        start = pl.multiple_of(j * tk, tk)
        ks = k_ref[0, pl.ds(start, tk), :].astype(BF16)
        vs = v_ref[0, pl.ds(start, tk), :].astype(BF16)
        fk = ft_ref[0, hd, pl.ds(j, 1), :]
        s = _dot_nt(q, ks) + fq - fk
        if masked:
            s = jnp.where(col + j * tk <= row, s, NEG_BIG)
        m_new = jnp.maximum(m, jnp.max(s, axis=-1, keepdims=True))
        alpha = jnp.exp(m - m_new)
        p = jnp.exp(s - m_new)
        l = alpha * l + jnp.sum(p, axis=-1, keepdims=True)
        acc = alpha * acc + _dot(p.astype(BF16), vs)
        return m_new, l, acc

    m0 = jnp.full((tq, 1), NEG_BIG, F32)
    l0 = jnp.zeros((tq, 1), F32)
    a0 = jnp.zeros((tq, HEAD_DIM), F32)
    n_free = (i * tq) // tk
    carry = lax.fori_loop(0, n_free, functools.partial(body, masked=False), (m0, l0, a0))
    _, l, acc = lax.fori_loop(n_free, n_kb, functools.partial(body, masked=True), carry)
    o_ref[0] = (acc / l).astype(o_ref.dtype)


def _attn_prompt(q, k, v, f, ft4):
    b, t, w = k.shape
    n_heads = w // HEAD_DIM
    tq, tk = min(ATTN_Q_BLOCK, t), min(ATTN_K_BLOCK, t)
    return pl.pallas_call(
        functools.partial(_attn_kernel, tq=tq, tk=tk),
        grid=(b, n_heads, t // tq),
        in_specs=[pl.BlockSpec((1, tq, HEAD_DIM), lambda bi, h, i: (bi, i, h)),
                  pl.BlockSpec((1, t, HEAD_DIM), lambda bi, h, i: (bi, 0, h)),
                  pl.BlockSpec((1, t, HEAD_DIM), lambda bi, h, i: (bi, 0, h)),
                  pl.BlockSpec((1, tq, LANES), lambda bi, h, i: (bi, i, 0)),
                  pl.BlockSpec((1, n_heads, t // tk, tk), lambda bi, h, i: (bi, 0, 0, 0))],
        out_specs=pl.BlockSpec((1, tq, HEAD_DIM), lambda bi, h, i: (bi, i, h)),
        out_shape=jax.ShapeDtypeStruct((b, t, w), BF16),
        compiler_params=_params("arbitrary", "arbitrary", "arbitrary"),
        name="fox_prompt",
    )(q, k, v, f, ft4)


def _attn_sample_kernel(pt_ref, q_ref, kn_ref, vn_ref, lfn_ref, suall_ref, cu_ref, *rest, n_heads, pages_per_step):
    npg = pages_per_step
    kp_refs = rest[:npg]
    vp_refs = rest[npg:2 * npg]
    lp_refs = rest[2 * npg:3 * npg]
    o_ref = rest[3 * npg]
    m_scr, l_scr, acc_scr, carry_scr, colb_scr, cflat_scr = rest[3 * npg + 1:]
    g = pl.program_id(1)
    n_g = pl.num_programs(1)
    nr = q_ref.shape[1]
    pk = PAGE_SIZE * n_heads
    q = q_ref[0].astype(BF16)

    def head_match(cols):
        r = lax.broadcasted_iota(jnp.int32, (nr, cols), 0)
        c = lax.broadcasted_iota(jnp.int32, (nr, cols), 1)
        return r, c, (c % n_heads) == (r % n_heads)

    def online(logits, values):
        m_old = m_scr[...]
        m_new = m_old
        for s in logits:
            m_new = jnp.maximum(m_new, jnp.max(s, axis=-1, keepdims=True))
        alpha = jnp.exp(m_old - m_new)
        l_new = alpha * l_scr[...]
        acc = alpha * acc_scr[...]
        for s, vb in zip(logits, values):
            p = jnp.exp(s - m_new)
            l_new = l_new + jnp.sum(p, axis=-1, keepdims=True)
            acc = acc + _dot(p.astype(BF16), vb)
        l_scr[...] = l_new
        acc_scr[...] = acc
        m_scr[...] = m_new

    def stack3(x, rows):
        parts = [p.astype(F32) for p in _split3(x)]
        pad = rows - 3 * x.shape[0]
        return jnp.concatenate(parts + [jnp.zeros((pad, x.shape[1]), F32)], axis=0).astype(BF16)

    @pl.when(g == 0)
    def _():
        m_scr[...] = jnp.full(m_scr.shape, NEG_BIG, F32)
        l_scr[...] = jnp.zeros(l_scr.shape, F32)
        acc_scr[...] = jnp.zeros(acc_scr.shape, F32)
        carry_scr[...] = jnp.zeros(carry_scr.shape, F32)
        cs = _dot(stack3(lfn_ref[0], 2 * SUBLANES), cu_ref[...])
        cflat = cs[0:1, :] + cs[1:2, :] + cs[2:3, :]
        cflat_scr[...] = cflat
        r, c, _ = head_match(LANES)
        colb_scr[...] = jnp.sum(jnp.where(r == c, cflat, 0.0), axis=-1, keepdims=True)

    lf_rows = jnp.concatenate([lp_refs[r][0, 0] for r in range(npg)], axis=0)
    sums = _dot(stack3(lf_rows, 4 * npg), suall_ref[...])
    sums = sums[0:npg, :] + sums[npg:2 * npg, :] + sums[2 * npg:3 * npg, :]
    colb = colb_scr[...]
    _, _, hm = head_match(pk)
    carry = carry_scr[...]
    logits, values = [], []
    for r in range(npg):
        bias = sums[r:r + 1, 0:pk] + carry
        carry = carry + sums[r:r + 1, pk:2 * pk]
        kb = kp_refs[r][0, 0].reshape(pk, HEAD_DIM).astype(BF16)
        logits.append(jnp.where(hm, _dot_nt(q, kb) + bias + colb, NEG_BIG))
        values.append(vp_refs[r][0, 0].reshape(pk, HEAD_DIM).astype(BF16))
    online(logits, values)
    carry_scr[...] = carry

    @pl.when(g == n_g - 1)
    def _():
        zeros = jnp.zeros((LANES - nr, HEAD_DIM), F32)
        kb = jnp.concatenate([kn_ref[0], zeros], axis=0).astype(BF16)
        vb = jnp.concatenate([vn_ref[0], zeros], axis=0).astype(BF16)
        r, c, hmn = head_match(LANES)
        ok = hmn & (c // n_heads <= r // n_heads) & (c < nr)
        online([jnp.where(ok, _dot_nt(q, kb) + colb - cflat_scr[...], NEG_BIG)], [vb])
        o_ref[0] = (acc_scr[...] / l_scr[...]).astype(o_ref.dtype)


def _attn_sample(page_table, q, kn, vn, lfn, cache_k, cache_v, cache_lf, layer):
    nb, nr, _ = q.shape
    n_heads = cache_k.shape[3]
    pk = PAGE_SIZE * n_heads
    n_pages = page_table.shape[1]
    npg = PAGES_PER_STEP
    while n_pages % npg:
        npg //= 2
    assert nr <= LANES and npg % 4 == 0

    idx = jnp.arange(pk)
    same = (idx[:, None] % n_heads) == (idx[None, :] % n_heads)
    later = (idx[:, None] // n_heads) > (idx[None, :] // n_heads)
    suall = jnp.concatenate([same & later, same], axis=1).astype(BF16)
    il = jnp.arange(LANES)
    cu = ((il[:, None] % n_heads == il[None, :] % n_heads) & (il[:, None] // n_heads <= il[None, :] // n_heads)
          & (il[:, None] < nr) & (il[None, :] < nr)).astype(BF16)

    def page_spec(r, last):
        def imap(b, g, pt):
            return (layer, pt[b, n_pages - 1 - (g * npg + r)]) + (0,) * len(last)
        return pl.BlockSpec((1, 1) + last, imap)

    new_spec = lambda rows, last: pl.BlockSpec((1, rows, last), lambda b, g, pt: (b, 0, 0))
    const = lambda shape: pl.BlockSpec(shape, lambda b, g, pt: (0, 0))
    in_specs = ([new_spec(nr, HEAD_DIM)] * 3 + [new_spec(1, LANES), const(suall.shape), const(cu.shape)]
                + [page_spec(r, (PAGE_SIZE, n_heads, HEAD_DIM)) for r in range(npg)]
                + [page_spec(r, (PAGE_SIZE, n_heads, HEAD_DIM)) for r in range(npg)]
                + [page_spec(r, (1, pk)) for r in range(npg)])
    grid_spec = pltpu.PrefetchScalarGridSpec(
        num_scalar_prefetch=1,
        grid=(nb, n_pages // npg),
        in_specs=in_specs,
        out_specs=pl.BlockSpec((1, nr, HEAD_DIM), lambda b, g, pt: (b, 0, 0)),
        scratch_shapes=[pltpu.VMEM((nr, 1), F32), pltpu.VMEM((nr, 1), F32), pltpu.VMEM((nr, HEAD_DIM), F32),
                        pltpu.VMEM((1, pk), F32), pltpu.VMEM((nr, 1), F32), pltpu.VMEM((1, LANES), F32)])
    return pl.pallas_call(
        functools.partial(_attn_sample_kernel, n_heads=n_heads, pages_per_step=npg),
        grid_spec=grid_spec,
        out_shape=jax.ShapeDtypeStruct((nb, nr, HEAD_DIM), BF16),
        compiler_params=_params("arbitrary", "arbitrary"),
        name="fox_sample",
    )(page_table, q, kn, vn, lfn, suall, cu, *([cache_k] * npg), *([cache_v] * npg), *([cache_lf] * npg))


def _ag_out_kernel(attn_ref, u_ref, gv_ref, mix_ref, bias_ref, wa_ref, wg_ref, x_ref, gate_ref,
                   o_ref, gm_scr, *, n_groups, chunk):
    tm = x_ref.shape[0]
    for g in range(n_groups):
        cols = slice(g * LANES, (g + 1) * LANES)
        for c in range(tm // chunk):
            rows = slice(c * chunk, (c + 1) * chunk)
            s = _dot(mix_ref[g], gv_ref[rows, cols].astype(BF16)) + bias_ref[rows, cols]
            gm_scr[rows, cols] = (u_ref[rows, cols].astype(F32) * s).astype(BF16)
    out = _dot(attn_ref[...], wa_ref[...]) + _dot(gm_scr[...], wg_ref[...])
    o_ref[...] = x_ref[...] + gate_ref[0] * out


def _ag_out(attn, u, gv, mix, bias, wa, wg, x, gate, *, blocks_per_group):
    n, d = x.shape
    w = attn.shape[1]
    tm = ROW_BLOCK
    n_groups, chunk, _ = mix.shape
    row = lambda i: (i, 0)
    const2 = lambda i: (0, 0)
    return pl.pallas_call(
        functools.partial(_ag_out_kernel, n_groups=n_groups, chunk=chunk),
        grid=(n // tm,),
        in_specs=[pl.BlockSpec((tm, w), row), pl.BlockSpec((tm, w), row), pl.BlockSpec((tm, w), row),
                  pl.BlockSpec(mix.shape, lambda i: (0, 0, 0)),
                  pl.BlockSpec((tm, w), const2),
                  pl.BlockSpec((w, d), const2), pl.BlockSpec((w, d), const2),
                  pl.BlockSpec((tm, d), row),
                  _mod_spec(gate, blocks_per_group)],
        out_specs=pl.BlockSpec((tm, d), row),
        out_shape=jax.ShapeDtypeStruct((n, d), F32),
        scratch_shapes=[pltpu.VMEM((tm, w), BF16)],
        compiler_params=_params("arbitrary"),
        name="ag_out",
    )(attn, u, gv, mix, bias, wa, wg, x, gate)


def _conv_in_kernel(x_ref, g_ref, sh_ref, sc_ref, wb_ref, wc_ref, wz_ref, bg_ref, p_ref, h_scr):
    @pl.when(pl.program_id(1) == 0)
    def _():
        h_scr[...] = _modnorm(x_ref[...], g_ref[...], sh_ref[0], sc_ref[0]).astype(BF16)

    h = h_scr[...]
    bg_ref[...] = _dot(h, wb_ref[...]).astype(bg_ref.dtype)
    p_ref[...] = _dot(h, wc_ref[...]) * _dot(h, wz_ref[...])


def _conv_in(x, g, shift, scale, w3, *, blocks_per_group):
    n, d = x.shape
    tm, tn = ROW_BLOCK, CONV_COL_BLOCK
    nj = d // tn
    row = lambda i, j: (i, 0)
    return pl.pallas_call(
        _conv_in_kernel,
        grid=(n // tm, nj),
        in_specs=[pl.BlockSpec((tm, d), row),
                  pl.BlockSpec((1, d), lambda i, j: (0, 0)),
                  _mod_spec(shift, blocks_per_group),
                  _mod_spec(scale, blocks_per_group),
                  pl.BlockSpec((d, tn), lambda i, j: (0, j)),
                  pl.BlockSpec((d, tn), lambda i, j: (0, j + nj)),
                  pl.BlockSpec((d, tn), lambda i, j: (0, j + 2 * nj))],
        out_specs=[pl.BlockSpec((tm, tn), lambda i, j: (i, j))] * 2,
        out_shape=(jax.ShapeDtypeStruct((n, d), BF16), jax.ShapeDtypeStruct((n, d), F32)),
        scratch_shapes=[pltpu.VMEM((tm, d), BF16)],
        compiler_params=_params("arbitrary", "arbitrary"),
        name="conv_in",
    )(x, g, shift, scale, w3, w3, w3)


def _conv_out_kernel(p_ref, s1_ref, s2_ref, bg_ref, cw_ref, cb_ref, wo_ref, x_ref, gate_ref, o_ref, *, period):
    p = p_ref[...]
    t = lax.broadcasted_iota(jnp.int32, p.shape, 0) % period
    if period == p.shape[0]:
        s1 = jnp.broadcast_to(s1_ref[0, 0:1, :], p.shape)
        s2 = jnp.where(t == 0, jnp.broadcast_to(s2_ref[0, 0:1, :], p.shape),
                       jnp.broadcast_to(s2_ref[0, 1:2, :], p.shape))
    else:
        s1, s2 = s1_ref[...], s2_ref[...]
    p1 = jnp.where(t == 0, s1, pltpu.roll(p, 1, axis=0))
    p2 = jnp.where(t < 2, s2, pltpu.roll(p, 2, axis=0))
    y = cb_ref[...] + cw_ref[0:1, :] * p2
    y = y + cw_ref[1:2, :] * p1
    y = y + cw_ref[2:3, :] * p
    out = _dot((bg_ref[...].astype(F32) * y).astype(BF16), wo_ref[...])
    o_ref[...] = x_ref[...] + gate_ref[0] * out


def _conv_out(p, s1, s2, bg, cw, cb, wo, x, gate, *, blocks_per_group, period):
    n, d = x.shape
    tm = CONV_ROW_BLOCK
    row = lambda i: (i, 0)
    const2 = lambda i: (0, 0)
    if period == tm:
        s_spec = pl.BlockSpec((1, SUBLANES, d), lambda i: (i, 0, 0))
    else:
        s_spec = pl.BlockSpec((tm, d), row)
    return pl.pallas_call(
        functools.partial(_conv_out_kernel, period=period),
        grid=(n // tm,),
        in_specs=[pl.BlockSpec((tm, d), row), s_spec, s_spec, pl.BlockSpec((tm, d), row),
                  pl.BlockSpec(cw.shape, const2), pl.BlockSpec((1, d), const2),
                  pl.BlockSpec((d, d), const2), pl.BlockSpec((tm, d), row),
                  _mod_spec(gate, blocks_per_group)],
        out_specs=pl.BlockSpec((tm, d), row),
        out_shape=jax.ShapeDtypeStruct((n, d), F32),
        compiler_params=_params("arbitrary"),
        name="conv_out",
    )(p, s1, s2, bg, cw, cb, wo, x, gate)


def _batcher_pairs(n):
    pairs = []
    p = 1
    while p < n:
        k = p
        while k >= 1:
            for j in range(k % p, n - k, 2 * k):
                for i in range(min(k, n - j - k)):
                    if (i + j) // (2 * p) == (i + j + k) // (2 * p):
                        pairs.append((i + j, i + j + k))
            k //= 2
        p *= 2
    return pairs


def _exchange(v, i, j):
    v[i], v[j] = jnp.maximum(v[i], v[j]), jnp.minimum(v[i], v[j])


def _merge_top(a, b):
    n = len(a)
    c = [jnp.maximum(a[k], b[n - 1 - k]) for k in range(n)]
    d = n // 2
    while d >= 1:
        for i in range(n):
            if i & d == 0:
                _exchange(c, i, i + d)
        d //= 2
    return c


def _merge_sublanes(v):
    sh = SUBLANES // 2
    while sh >= 1:
        v = _merge_top(v, [pltpu.roll(x, sh, axis=0) for x in v])
        sh //= 2
    return v


def _top_sorted(s_ref):
    v = [s_ref[k * SUBLANES:(k + 1) * SUBLANES, :] for k in range(N_KEYS // SUBLANES)]
    for i, j in _batcher_pairs(len(v)):
        _exchange(v, i, j)
    return _merge_sublanes(v)


def _route_kernel(x_ref, g_ref, sh_ref, sc_ref, wqt_ref, k1_ref, k2_ref,
                  xt_ref, e1_ref, e2_ref, th_ref, qt_scr, s1_scr, s2_scr):
    assert N_KEYS // SUBLANES == PEER_TOPK
    tm = x_ref.shape[0]
    ht = _modnorm(x_ref[...], g_ref[...], sh_ref[0], sc_ref[0]).T.astype(BF16)
    xt_ref[...] = ht
    qt_scr[...] = _dot(wqt_ref[...], ht)
    d_key = 2 * N_KEYS
    sub = lax.broadcasted_iota(jnp.int32, (SUBLANES, tm), 0)

    def pack_rows(vals):
        out = vals[0]
        for s in range(1, SUBLANES):
            out = jnp.where(sub == s, vals[s], out)
        return out

    def head_body(hd, _):
        base = pl.multiple_of(hd * d_key, d_key)
        s1_scr[...] = _dot(k1_ref[...], qt_scr[pl.ds(base, N_KEYS), :].astype(BF16))
        s2_scr[...] = _dot(k2_ref[...], qt_scr[pl.ds(base + N_KEYS, N_KEYS), :].astype(BF16))
        t1 = _top_sorted(s1_scr)
        t2 = _top_sorted(s2_scr)
        a_lo = pack_rows(t1[:SUBLANES])
        a_hi = pack_rows(t1[SUBLANES:])
        lo = [a_lo + t for t in t2]
        hi = [a_hi + t for t in t2]
        top = _merge_sublanes(_merge_top(lo, hi))
        g16 = top[PEER_TOPK - 1]
        g17 = jnp.full((SUBLANES, tm), -jnp.inf, F32)
        for c in lo + hi:
            g17 = jnp.maximum(g17, jnp.where(c < g16, c, -jnp.inf))
        sh = SUBLANES // 2
        while sh >= 1:
            g17 = jnp.maximum(g17, pltpu.roll(g17, sh, axis=0))
            sh //= 2
        z = jnp.zeros((SUBLANES, tm), F32)
        for t in top:
            z = z + jnp.exp(t - top[0])
        row = lambda a: a[0:1, :]
        tau = 0.5 * (row(g16) + row(g17))
        s1 = s1_scr[...]
        s2 = s2_scr[...]
        e1_ref[hd] = jnp.where(s1 >= row(t1[PEER_TOPK - 1]), jnp.exp(s1 - row(t1[0])) * (0.5 / row(z)), 0.0)
        e2_ref[hd] = jnp.where(s2 >= row(t2[PEER_TOPK - 1]), jnp.exp(s2 - row(t2[0])), 0.0)
        th_ref[hd] = jnp.exp((tau - row(t2[0])) - s1)
        return 0

    lax.fori_loop(0, PEER_HEADS, head_body, 0)


def _route(x, g, shift, scale, wqt, k1, k2, *, blocks_per_group):
    n, d = x.shape
    tm = ROUTE_BLOCK
    qw = wqt.shape[0]
    row = lambda i: (i, 0)
    const2 = lambda i: (0, 0)
    fac = jax.ShapeDtypeStruct((PEER_HEADS, N_KEYS, n), F32)
    fac_spec = pl.BlockSpec((PEER_HEADS, N_KEYS, tm), lambda i: (0, 0, i))
    key_scr = pltpu.VMEM((N_KEYS, tm), F32)
    return pl.pallas_call(
        _route_kernel,
        grid=(n // tm,),
        in_specs=[pl.BlockSpec((tm, d), row), pl.BlockSpec((1, d), const2),
                  _mod_spec(shift, blocks_per_group * (ROW_BLOCK // tm)),
                  _mod_spec(scale, blocks_per_group * (ROW_BLOCK // tm)),
                  pl.BlockSpec((qw, d), const2),
                  pl.BlockSpec((N_KEYS, N_KEYS), const2), pl.BlockSpec((N_KEYS, N_KEYS), const2)],
        out_specs=[pl.BlockSpec((d, tm), lambda i: (0, i)), fac_spec, fac_spec, fac_spec],
        out_shape=(jax.ShapeDtypeStruct((d, n), BF16), fac, fac, fac),
        scratch_shapes=[pltpu.VMEM((qw, tm), F32), key_scr, key_scr],
        compiler_params=_params("arbitrary"),
        name="peer_route",
    )(x, g, shift, scale, wqt, k1, k2)


def _gelu_twice(a):
    c = 0.7978845608028654
    return a * (1.0 + jnp.tanh(a * (c + (c * 0.044715) * (a * a))))


def _peer_gate_piece(a_scr, h_scr, e1_ref, e2_ref, th_ref, chunk, iis, jrows, cols, key_rows):
    te = a_scr.shape[0]
    for ii in iis:
        if ii not in key_rows:
            i_glob = chunk * (te // N_KEYS) + ii
            key_rows[ii] = [(th_ref[hd, pl.ds(i_glob, 1), :], e1_ref[hd, pl.ds(i_glob, 1), :])
                            for hd in range(PEER_HEADS)]
    gates = [None] * len(iis)
    for hd in range(PEER_HEADS):
        e2 = e2_ref[hd, jrows, cols]
        for n, ii in enumerate(iis):
            th, e1 = key_rows[ii][hd]
            term = jnp.where(e2 >= th[:, cols], e2, 0.0) * e1[:, cols]
            gates[n] = term if gates[n] is None else gates[n] + term
    for n, ii in enumerate(iis):
        rows = slice(ii * N_KEYS + jrows.start, ii * N_KEYS + jrows.stop)
        h_scr[rows, cols] = (_gelu_twice(a_scr[rows, cols]) * gates[n]).astype(BF16)


def _peer_kernel(xt_ref, u_ref, vt_ref, e1_ref, e2_ref, th_ref, yt_ref, a_scr, h_scr):
    c = pl.program_id(1)
    te, tm = a_scr.shape

    @pl.when(c == 0)
    def _():
        yt_ref[...] = jnp.zeros(yt_ref.shape, F32)

    a_scr[...] = _dot(u_ref[...], xt_ref[...])
    key_rows = {}
    half = N_KEYS // 2
    for ii in range(0, te // N_KEYS, 2):
        for k in range(tm // LANES):
            for jrows in (slice(0, half), slice(half, N_KEYS)):
                _peer_gate_piece(a_scr, h_scr, e1_ref, e2_ref, th_ref, c, (ii, ii + 1), jrows,
                                 slice(k * LANES, (k + 1) * LANES), key_rows)
    yt_ref[...] += _dot(vt_ref[...], h_scr[...])


def _peer(xt, u, vt, e1, e2, th):
    d, n = xt.shape
    ne = u.shape[0]
    tm, te = ROW_BLOCK, EXPERT_BLOCK
    fac_spec = pl.BlockSpec((PEER_HEADS, N_KEYS, tm), lambda i, c: (0, 0, i))
    return pl.pallas_call(
        _peer_kernel,
        grid=(n // tm, ne // te),
        in_specs=[pl.BlockSpec((d, tm), lambda i, c: (0, i)),
                  pl.BlockSpec((te, d), lambda i, c: (c, 0)),
                  pl.BlockSpec((d, te), lambda i, c: (0, c)),
                  fac_spec, fac_spec, fac_spec],
        out_specs=pl.BlockSpec((d, tm), lambda i, c: (0, i)),
        out_shape=jax.ShapeDtypeStruct((d, n), F32),
        scratch_shapes=[pltpu.VMEM((te, tm), F32), pltpu.VMEM((te, tm), BF16)],
        compiler_params=_params("arbitrary", "arbitrary"),
        name="peer_dense",
    )(xt, u, vt, e1, e2, th)


def _resid_kernel(x_ref, yt_ref, gate_ref, o_ref):
    o_ref[...] = x_ref[...] + gate_ref[0] * yt_ref[...].T


def _resid(x, yt, gate, *, blocks_per_group):
    n, d = x.shape
    tm = ROW_BLOCK
    row = lambda i: (i, 0)
    return pl.pallas_call(
        _resid_kernel,
        grid=(n // tm,),
        in_specs=[pl.BlockSpec((tm, d), row), pl.BlockSpec((d, tm), lambda i: (0, i)),
                  _mod_spec(gate, blocks_per_group)],
        out_specs=pl.BlockSpec((tm, d), row),
        out_shape=jax.ShapeDtypeStruct((n, d), F32),
        compiler_params=_params("arbitrary"),
        name="peer_resid",
    )(x, yt, gate)


class _Stream:
    def __init__(self, x, seq):
        self.batch = x.shape[0]
        self.seq = seq
        self.x = x.reshape(-1, x.shape[-1])
        self.long = seq % ROW_BLOCK == 0
        self.blocks_per_group = seq // ROW_BLOCK if self.long else 1

    def mod(self, m, k, tm=ROW_BLOCK):
        mk = m[:, k, :]
        if self.long:
            return mk[:, None, :]
        d = mk.shape[-1]
        return jnp.repeat(mk, self.seq, axis=0).reshape(-1, tm, d)


def kernel(x_prompt, x_sample, c_prompt, c_sample, cache_k, cache_v, cache_logf, page_table, state_conv,
           w_ada, b_ada, norm_mix, norm_ffn, w_in_ag, b_fgate, q_norm, k_norm, w_spatial, b_spatial,
           w_out_ag, w_in_conv, conv_w, conv_b, w_out_conv, peer_wq, peer_keys, peer_u, peer_v):
    bp, tp, d = x_prompt.shape
    bs, ts, _ = x_sample.shape
    depth = w_ada.shape[0]
    att_w = d // 2
    n_heads = att_w // HEAD_DIM
    n_groups = att_w // LANES
    assert tp % ROW_BLOCK == 0 and (bs * ts) % ROW_BLOCK == 0 and ROW_BLOCK % ts == 0 and ts == SUBLANES

    streams = [_Stream(x_prompt, tp), _Stream(x_sample, ts)]
    mod_all = _ada(jnp.concatenate([c_prompt, c_sample], axis=0), w_ada, b_ada)
    ck, cv = cache_k, cache_v
    clf = cache_logf.reshape(cache_logf.shape[0], cache_logf.shape[1], 1, PAGE_SIZE * n_heads)

    outs = {"k": [[], []], "v": [[], []], "f": [[], []], "g": [], "c": [[], []]}
    for layer in range(depth):
        ml = mod_all[layer].reshape(bp + bs, N_MOD, d)
        mods = [ml[:bp], ml[bp:]]
        g_mix = norm_mix[layer][None, :]
        g_ffn = norm_ffn[layer][None, :]
        if layer % 2 == 0:
            a = layer // 2
            w_in = w_in_ag[a]
            w5 = jnp.concatenate([w_in[:, :3 * att_w], w_in[:, 3 * att_w + n_heads:]], axis=1).astype(BF16)
            wf = jnp.pad(w_in[:, 3 * att_w:3 * att_w + n_heads], ((0, 0), (0, LANES - n_heads))).astype(BF16)
            bf = jnp.pad(b_fgate[a], (0, LANES - n_heads))[None, :]
            wa = w_out_ag[a][:att_w].astype(BF16)
            wg = w_out_ag[a][att_w:].astype(BF16)
            tril = jnp.tril(w_spatial[a])
        else:
            ci = layer // 2
            w3 = w_in_conv[ci].astype(BF16)
            wo = w_out_conv[ci].astype(BF16)
        wqt = peer_wq[layer].T.astype(BF16)
        k1 = peer_keys[layer, 0].astype(BF16)
        k2 = peer_keys[layer, 1].astype(BF16)
        pu = peer_u[layer].astype(BF16)
        pvt = peer_v[layer].T.astype(BF16)

        for si, st in enumerate(streams):
            m = mods[si]
            bpg = st.blocks_per_group
            if layer % 2 == 0:
                q, k, v, u, gv, lf = _ag_in(st.x, g_mix, st.mod(m, 0), st.mod(m, 1), w5, wf, bf,
                                            q_norm[a][None, :], k_norm[a][None, :],
                                            blocks_per_group=bpg, q_dtype=BF16 if st.long else F32)
                if st.long:
                    f, ft = _cumsum(lf.reshape(st.batch, st.seq, LANES))
                    tk = min(ATTN_K_BLOCK, st.seq)
                    ft4 = ft[:, :n_heads, :].reshape(st.batch, n_heads, st.seq // tk, tk)
                    attn = _attn_prompt(q.reshape(st.batch, st.seq, att_w), k.reshape(st.batch, st.seq, att_w),
                                        v.reshape(st.batch, st.seq, att_w), f, ft4)
                    mix = tril.astype(BF16)
                    bias = jnp.tile(jnp.repeat(b_spatial[a].T, LANES, axis=1), (ROW_BLOCK // CHUNK, 1))
                else:
                    nr = st.seq * n_heads
                    lfn = jnp.pad(lf[:, :n_heads].reshape(st.batch, 1, nr), ((0, 0), (0, 0), (0, LANES - nr)))
                    attn = _attn_sample(page_table, q.reshape(st.batch, nr, HEAD_DIM),
                                        k.reshape(st.batch, nr, HEAD_DIM), v.reshape(st.batch, nr, HEAD_DIM),
                                        lfn, ck, cv, clf, a)
                    eye = jnp.eye(ROW_BLOCK // st.seq, dtype=F32)
                    mix = jax.vmap(lambda t: jnp.kron(eye, t))(tril[:, :st.seq, :st.seq]).astype(BF16)
                    bias = jnp.tile(jnp.repeat(b_spatial[a].T[:st.seq], LANES, axis=1), (ROW_BLOCK // st.seq, 1))
                    outs["g"].append(gv.reshape(st.batch, st.seq, att_w))
                st.x = _ag_out(attn.reshape(-1, att_w), u, gv, mix, bias, wa, wg, st.x, st.mod(m, 2),
                               blocks_per_group=bpg)
                outs["k"][si].append(k.reshape(st.batch, st.seq, n_heads, HEAD_DIM))
                outs["v"][si].append(v.reshape(st.batch, st.seq, n_heads, HEAD_DIM))
                outs["f"][si].append(lf[:, :n_heads].reshape(st.batch, st.seq, n_heads))
            else:
                bg, p = _conv_in(st.x, g_mix, st.mod(m, 0), st.mod(m, 1), w3, blocks_per_group=bpg)
                p3 = p.reshape(st.batch, st.seq, d)
                tc = CONV_ROW_BLOCK
                if st.long:
                    cb = st.seq // tc
                    pb = p.reshape(st.batch, cb, tc, d)[:, :, tc - SUBLANES:, :]
                    prev = jnp.concatenate([jnp.zeros_like(pb[:, :1]), pb[:, :-1]], axis=1)
                    prev = prev.reshape(st.batch * cb, SUBLANES, d)
                    s2 = prev[:, SUBLANES - 2:, :]
                    s2 = jnp.pad(s2, ((0, 0), (0, SUBLANES - 2), (0, 0)))
                    s1 = jnp.pad(prev[:, SUBLANES - 1:, :], ((0, 0), (0, SUBLANES - 1), (0, 0)))
                    period = tc
                else:
                    cb = 1
                    buf = state_conv[ci]
                    s2 = jnp.pad(buf, ((0, 0), (0, st.seq - 2), (0, 0))).reshape(-1, d)
                    s1 = jnp.pad(buf[:, 1:], ((0, 0), (0, st.seq - 1), (0, 0))).reshape(-1, d)
                    period = st.seq
                st.x = _conv_out(p, s1, s2, bg, conv_w[ci], conv_b[ci][None, :], wo, st.x, st.mod(m, 2, tc),
                                 blocks_per_group=cb, period=period)
                outs["c"][si].append(p3[:, st.seq - 2:, :])
            xt, e1, e2, th = _route(st.x, g_ffn, st.mod(m, 3, ROUTE_BLOCK), st.mod(m, 4, ROUTE_BLOCK),
                                    wqt, k1, k2, blocks_per_group=bpg)
            yt = _peer(xt, pu, pvt, e1, e2, th)
            st.x = _resid(st.x, yt, st.mod(m, 5), blocks_per_group=bpg)

    sp, ss = streams
    return (sp.x.reshape(bp, tp, d), ss.x.reshape(bs, ts, d),
            jnp.stack(outs["k"][0]), jnp.stack(outs["v"][0]), jnp.stack(outs["f"][0]),
            jnp.stack(outs["k"][1]), jnp.stack(outs["v"][1]), jnp.stack(outs["f"][1]),
            jnp.stack(outs["g"]), jnp.stack(outs["c"][0]), jnp.stack(outs["c"][1]))
```

```python
import functools

import jax
import jax.numpy as jnp
from jax import lax
from jax.experimental import pallas as pl
from jax.experimental.pallas import tpu as pltpu

F32 = jnp.float32
BF16 = jnp.bfloat16

EPS = 1e-6
LANES = 128
SUBLANES = 8
BF16_SUBLANES = 16
HEAD_DIM = 128
N_MOD = 6
PEER_HEADS = 8
PEER_TOPK = 16
N_KEYS = 128
PAGE_SIZE = 128
CHUNK = 128
VMEM_LIMIT_BYTES = 56 * 1024 * 1024
NEG_BIG = -1e30
ROW_BLOCK = 512
ROUTE_BLOCK = 256
EXPERT_BLOCK = 512
PAGES_PER_STEP = 16
ATTN_K_BLOCK = 512
ADA_COL_BLOCK = 1024
CONV_COL_BLOCK = 512
CONV_ROW_BLOCK = 256


def _params(*sem):
    return pltpu.CompilerParams(dimension_semantics=sem, vmem_limit_bytes=VMEM_LIMIT_BYTES)


def _modnorm(x, g, shift, scale):
    y = x * lax.rsqrt(jnp.mean(x * x, axis=-1, keepdims=True) + EPS)
    return (y * g) * (1.0 + scale) + shift


def _gelu(a):
    return jax.nn.gelu(a)


def _log_sigmoid(x):
    return jnp.minimum(x, 0.0) - jnp.log1p(jnp.exp(-jnp.abs(x)))


def _split3(x):
    hi = x.astype(BF16)
    r = x - hi.astype(F32)
    mid = r.astype(BF16)
    lo = (r - mid.astype(F32)).astype(BF16)
    return hi, mid, lo


def _dot(a, b):
    return jnp.dot(a, b, preferred_element_type=F32)


def _dot_nt(a, b):
    return lax.dot_general(a, b, (((1,), (1,)), ((), ())), preferred_element_type=F32)


def _mod_spec(mod, blocks_per_group):
    _, r, d = mod.shape
    if r == 1:
        return pl.BlockSpec((1, 1, d), lambda i, *_: (i // blocks_per_group, 0, 0))
    return pl.BlockSpec((1, r, d), lambda i, *_: (i, 0, 0))


def _ada_kernel(c_ref, w_ref, b_ref, o_ref):
    c = c_ref[...]
    s = (c * jax.nn.sigmoid(c)).astype(BF16)
    o_ref[0] = _dot(s, w_ref[0].astype(BF16)) + b_ref[0]


def _ada(c_all, w_ada, b_ada):
    depth, d, n6 = w_ada.shape
    r = c_all.shape[0]
    tn = ADA_COL_BLOCK
    return pl.pallas_call(
        _ada_kernel,
        grid=(depth, n6 // tn),
        in_specs=[pl.BlockSpec((r, d), lambda l, j: (0, 0)),
                  pl.BlockSpec((1, d, tn), lambda l, j: (l, 0, j)),
                  pl.BlockSpec((1, 1, tn), lambda l, j: (l, 0, j))],
        out_specs=pl.BlockSpec((1, r, tn), lambda l, j: (l, 0, j)),
        out_shape=jax.ShapeDtypeStruct((depth, r, n6), F32),
        compiler_params=_params("arbitrary", "arbitrary"),
        name="ada_mod",
    )(c_all, w_ada, b_ada.reshape(depth, 1, n6))


def _ag_in_kernel(x_ref, g_ref, sh_ref, sc_ref, w_ref, wf_ref, bf_ref, qg_ref, kg_ref,
                  q_ref, k_ref, v_ref, u_ref, gv_ref, lf_ref, h_scr, *, n_heads):
    j = pl.program_id(1)

    @pl.when(j == 0)
    def _():
        h = _modnorm(x_ref[...], g_ref[...], sh_ref[0], sc_ref[0]).astype(BF16)
        h_scr[...] = h
        lf_ref[...] = _log_sigmoid(_dot(h, wf_ref[...]) + bf_ref[...])

    def z():
        return _dot(h_scr[...], w_ref[...])

    def head_norm(zz, g, o_ref, post):
        for hd in range(n_heads):
            zh = zz[:, hd * HEAD_DIM:(hd + 1) * HEAD_DIM]
            y = zh * lax.rsqrt(jnp.mean(zh * zh, axis=-1, keepdims=True) + EPS) * g
            o_ref[:, hd * HEAD_DIM:(hd + 1) * HEAD_DIM] = (y * post).astype(o_ref.dtype)

    @pl.when(j == 0)
    def _():
        head_norm(z(), qg_ref[...], q_ref, HEAD_DIM ** -0.5)

    @pl.when(j == 1)
    def _():
        head_norm(z(), kg_ref[...], k_ref, 1.0)

    @pl.when(j == 2)
    def _():
        v_ref[...] = z()

    @pl.when(j == 3)
    def _():
        u_ref[...] = _gelu(z()).astype(u_ref.dtype)

    @pl.when(j == 4)
    def _():
        gv_ref[...] = _gelu(z())


def _ag_in(x, g, shift, scale, w5, wf, bf, qg, kg, *, blocks_per_group, q_dtype):
    n, d = x.shape
    w = w5.shape[1] // 5
    tm = ROW_BLOCK
    row = lambda i, j: (i, 0)
    const = lambda i, j: (0, 0)
    out_shapes = (jax.ShapeDtypeStruct((n, w), q_dtype),
                  jax.ShapeDtypeStruct((n, w), F32),
                  jax.ShapeDtypeStruct((n, w), F32),
                  jax.ShapeDtypeStruct((n, w), BF16),
                  jax.ShapeDtypeStruct((n, w), F32),
                  jax.ShapeDtypeStruct((n, LANES), F32))
    return pl.pallas_call(
        functools.partial(_ag_in_kernel, n_heads=w // HEAD_DIM),
        grid=(n // tm, 5),
        in_specs=[pl.BlockSpec((tm, d), row),
                  pl.BlockSpec((1, d), const),
                  _mod_spec(shift, blocks_per_group),
                  _mod_spec(scale, blocks_per_group),
                  pl.BlockSpec((d, w), lambda i, j: (0, j)),
                  pl.BlockSpec((d, LANES), const),
                  pl.BlockSpec((1, LANES), const),
                  pl.BlockSpec((1, HEAD_DIM), const),
                  pl.BlockSpec((1, HEAD_DIM), const)],
        out_specs=[pl.BlockSpec((tm, w), row)] * 5 + [pl.BlockSpec((tm, LANES), row)],
        out_shape=out_shapes,
        scratch_shapes=[pltpu.VMEM((tm, d), BF16)],
        compiler_params=_params("arbitrary", "arbitrary"),
        name="ag_in",
    )(x, g, shift, scale, w5, wf, bf, qg, kg)


def _tri_ones(n):
    r = lax.broadcasted_iota(jnp.int32, (n, n), 0)
    c = lax.broadcasted_iota(jnp.int32, (n, n), 1)
    return jnp.where(r >= c, 1.0, 0.0).astype(BF16)


def _cumsum_kernel(lf_ref, f_ref, ft_ref):
    t = lf_ref.shape[1]
    tri = _tri_ones(CHUNK)
    carry = jnp.zeros((1, LANES), F32)
    for c in range(t // CHUNK):
        hi, mid, lo = _split3(lf_ref[0, c * CHUNK:(c + 1) * CHUNK, :])
        cs = (_dot(tri, hi) + _dot(tri, mid)) + _dot(tri, lo) + carry
        f_ref[0, c * CHUNK:(c + 1) * CHUNK, :] = cs
        carry = cs[CHUNK - 1:CHUNK, :]
    ft_ref[0] = f_ref[0].T


def _cumsum(lf):
    b, t, _ = lf.shape
    return pl.pallas_call(
        _cumsum_kernel,
        grid=(b,),
        in_specs=[pl.BlockSpec((1, t, LANES), lambda i: (i, 0, 0))],
        out_specs=[pl.BlockSpec((1, t, LANES), lambda i: (i, 0, 0)),
                   pl.BlockSpec((1, LANES, t), lambda i: (i, 0, 0))],
        out_shape=(jax.ShapeDtypeStruct((b, t, LANES), F32), jax.ShapeDtypeStruct((b, LANES, t), F32)),
        compiler_params=_params("arbitrary"),
        name="fox_cumsum",
    )(lf)


def _attn_kernel(q_ref, k_ref, v_ref, f_ref, ft_ref, o_ref, *, tb):
    hd = pl.program_id(1)
    t = q_ref.shape[1]
    nb = t // tb
    lane = lax.broadcasted_iota(jnp.int32, (tb, LANES), 1)
    row = lax.broadcasted_iota(jnp.int32, (tb, tb), 0)
    col = lax.broadcasted_iota(jnp.int32, (tb, tb), 1)
    kb = [k_ref[0, j * tb:(j + 1) * tb, :].astype(BF16) for j in range(nb)]
    vb = [v_ref[0, j * tb:(j + 1) * tb, :].astype(BF16) for j in range(nb)]
    fk = [ft_ref[0, hd, j:j + 1, :] for j in range(nb)]
    for i in range(nb):
        rows = slice(i * tb, (i + 1) * tb)
        q = q_ref[0, rows, :]
        fq = jnp.sum(jnp.where(lane == hd, f_ref[0, rows, :], 0.0), axis=-1, keepdims=True)
        logits = []
        for j in range(i + 1):
            s = _dot_nt(q, kb[j]) + fq - fk[j]
            logits.append(jnp.where(col <= row, s, NEG_BIG) if j == i else s)
        m = jnp.max(logits[0], axis=-1, keepdims=True)
        for s in logits[1:]:
            m = jnp.maximum(m, jnp.max(s, axis=-1, keepdims=True))
        l = jnp.zeros((tb, 1), F32)
        acc = jnp.zeros((tb, HEAD_DIM), F32)
        for j, s in enumerate(logits):
            p = jnp.exp(s - m)
            l = l + jnp.sum(p, axis=-1, keepdims=True)
            acc = acc + _dot(p.astype(BF16), vb[j])
        o_ref[0, rows, :] = (acc / l).astype(o_ref.dtype)


def _attn_prompt(q, k, v, f, ft4):
    b, t, w = k.shape
    n_heads = w // HEAD_DIM
    tb = ft4.shape[3]
    seq_spec = lambda last, col: pl.BlockSpec((1, t, last), lambda bi, h: (bi, 0, h if col else 0))
    return pl.pallas_call(
        functools.partial(_attn_kernel, tb=tb),
        grid=(b, n_heads),
        in_specs=[seq_spec(HEAD_DIM, True), seq_spec(HEAD_DIM, True), seq_spec(HEAD_DIM, True),
                  seq_spec(LANES, False),
                  pl.BlockSpec((1, n_heads, t // tb, tb), lambda bi, h: (bi, 0, 0, 0))],
        out_specs=seq_spec(HEAD_DIM, True),
        out_shape=jax.ShapeDtypeStruct((b, t, w), BF16),
        compiler_params=_params("arbitrary", "arbitrary"),
        name="fox_prompt",
    )(q, k, v, f, ft4)


def _attn_sample_kernel(pt_ref, q_ref, kn_ref, vn_ref, lfn_ref, suall_ref, cu_ref, *rest, n_heads, pages_per_step):
    npg = pages_per_step
    kp_refs = rest[:npg]
    vp_refs = rest[npg:2 * npg]
    lp_refs = rest[2 * npg:3 * npg]
    o_ref = rest[3 * npg]
    m_scr, l_scr, acc_scr, carry_scr, colb_scr, cflat_scr = rest[3 * npg + 1:]
    g = pl.program_id(1)
    n_g = pl.num_programs(1)
    nr = q_ref.shape[1]
    pk = PAGE_SIZE * n_heads
    q = q_ref[0].astype(BF16)

    def head_match(cols):
        r = lax.broadcasted_iota(jnp.int32, (nr, cols), 0)
        c = lax.broadcasted_iota(jnp.int32, (nr, cols), 1)
        return r, c, (c % n_heads) == (r % n_heads)

    def online(logits, values):
        m_old = m_scr[...]
        m_new = m_old
        for s in logits:
            m_new = jnp.maximum(m_new, jnp.max(s, axis=-1, keepdims=True))
        alpha = jnp.exp(m_old - m_new)
        l_new = alpha * l_scr[...]
        acc = alpha * acc_scr[...]
        for s, vb in zip(logits, values):
            p = jnp.exp(s - m_new)
            l_new = l_new + jnp.sum(p, axis=-1, keepdims=True)
            acc = acc + _dot(p.astype(BF16), vb)
        l_scr[...] = l_new
        acc_scr[...] = acc
        m_scr[...] = m_new

    def stack3(x):
        parts = [p.astype(F32) for p in _split3(x)]
        pad = -(3 * x.shape[0]) % BF16_SUBLANES
        if pad:
            parts.append(jnp.zeros((pad, x.shape[1]), F32))
        return jnp.concatenate(parts, axis=0).astype(BF16)

    @pl.when(g == 0)
    def _():
        m_scr[...] = jnp.full(m_scr.shape, NEG_BIG, F32)
        l_scr[...] = jnp.zeros(l_scr.shape, F32)
        acc_scr[...] = jnp.zeros(acc_scr.shape, F32)
        carry_scr[...] = jnp.zeros(carry_scr.shape, F32)
        cs = _dot(stack3(lfn_ref[0]), cu_ref[...])
        cflat = cs[0:1, :] + cs[1:2, :] + cs[2:3, :]
        cflat_scr[...] = cflat
        r, c, _ = head_match(LANES)
        colb_scr[...] = jnp.sum(jnp.where(r == c, cflat, 0.0), axis=-1, keepdims=True)

    lf_rows = jnp.concatenate([lp_refs[r][0, 0] for r in range(npg)], axis=0)
    sums = _dot(stack3(lf_rows), suall_ref[...])
    sums = sums[0:npg, :] + sums[npg:2 * npg, :] + sums[2 * npg:3 * npg, :]
    colb = colb_scr[...]
    _, _, hm = head_match(pk)
    carry = carry_scr[...]
    logits, values = [], []
    for r in range(npg):
        bias = sums[r:r + 1, 0:pk] + carry
        carry = carry + sums[r:r + 1, pk:2 * pk]
        kb = kp_refs[r][0, 0].reshape(pk, HEAD_DIM).astype(BF16)
        logits.append(jnp.where(hm, _dot_nt(q, kb) + bias + colb, NEG_BIG))
        values.append(vp_refs[r][0, 0].reshape(pk, HEAD_DIM).astype(BF16))
    online(logits, values)
    carry_scr[...] = carry

    @pl.when(g == n_g - 1)
    def _():
        zeros = jnp.zeros((LANES - nr, HEAD_DIM), F32)
        kb = jnp.concatenate([kn_ref[0], zeros], axis=0).astype(BF16)
        vb = jnp.concatenate([vn_ref[0], zeros], axis=0).astype(BF16)
        r, c, hmn = head_match(LANES)
        ok = hmn & (c // n_heads <= r // n_heads) & (c < nr)
        online([jnp.where(ok, _dot_nt(q, kb) + colb - cflat_scr[...], NEG_BIG)], [vb])
        o_ref[0] = (acc_scr[...] / l_scr[...]).astype(o_ref.dtype)


def _attn_sample(page_table, q, kn, vn, lfn, cache_k, cache_v, cache_lf, layer):
    nb, nr, _ = q.shape
    n_heads = cache_k.shape[3]
    pk = PAGE_SIZE * n_heads
    n_pages = page_table.shape[1]
    npg = PAGES_PER_STEP
    while n_pages % npg:
        npg //= 2
    assert nr <= LANES

    idx = jnp.arange(pk)
    same = (idx[:, None] % n_heads) == (idx[None, :] % n_heads)
    later = (idx[:, None] // n_heads) > (idx[None, :] // n_heads)
    suall = jnp.concatenate([same & later, same], axis=1).astype(BF16)
    il = jnp.arange(LANES)
    cu = ((il[:, None] % n_heads == il[None, :] % n_heads) & (il[:, None] // n_heads <= il[None, :] // n_heads)
          & (il[:, None] < nr) & (il[None, :] < nr)).astype(BF16)

    def page_spec(r, last):
        def imap(b, g, pt):
            return (layer, pt[b, n_pages - 1 - (g * npg + r)]) + (0,) * len(last)
        return pl.BlockSpec((1, 1) + last, imap)

    new_spec = lambda rows, last: pl.BlockSpec((1, rows, last), lambda b, g, pt: (b, 0, 0))
    const = lambda shape: pl.BlockSpec(shape, lambda b, g, pt: (0, 0))
    in_specs = ([new_spec(nr, HEAD_DIM)] * 3 + [new_spec(1, LANES), const(suall.shape), const(cu.shape)]
                + [page_spec(r, (PAGE_SIZE, n_heads, HEAD_DIM)) for r in range(npg)]
                + [page_spec(r, (PAGE_SIZE, n_heads, HEAD_DIM)) for r in range(npg)]
                + [page_spec(r, (1, pk)) for r in range(npg)])
    grid_spec = pltpu.PrefetchScalarGridSpec(
        num_scalar_prefetch=1,
        grid=(nb, n_pages // npg),
        in_specs=in_specs,
        out_specs=pl.BlockSpec((1, nr, HEAD_DIM), lambda b, g, pt: (b, 0, 0)),
        scratch_shapes=[pltpu.VMEM((nr, 1), F32), pltpu.VMEM((nr, 1), F32), pltpu.VMEM((nr, HEAD_DIM), F32),
                        pltpu.VMEM((1, pk), F32), pltpu.VMEM((nr, 1), F32), pltpu.VMEM((1, LANES), F32)])
    return pl.pallas_call(
        functools.partial(_attn_sample_kernel, n_heads=n_heads, pages_per_step=npg),
        grid_spec=grid_spec,
        out_shape=jax.ShapeDtypeStruct((nb, nr, HEAD_DIM), BF16),
        compiler_params=_params("arbitrary", "arbitrary"),
        name="fox_sample",
    )(page_table, q, kn, vn, lfn, suall, cu, *([cache_k] * npg), *([cache_v] * npg), *([cache_lf] * npg))


def _ag_out_kernel(attn_ref, u_ref, gv_ref, mix_ref, bias_ref, wa_ref, wg_ref, x_ref, gate_ref,
                   o_ref, gm_scr, *, n_groups, chunk):
    tm = x_ref.shape[0]
    for g in range(n_groups):
        cols = slice(g * LANES, (g + 1) * LANES)
        for c in range(tm // chunk):
            rows = slice(c * chunk, (c + 1) * chunk)
            s = _dot(mix_ref[g], gv_ref[rows, cols].astype(BF16)) + bias_ref[rows, cols]
            gm_scr[rows, cols] = (u_ref[rows, cols].astype(F32) * s).astype(BF16)
    out = _dot(attn_ref[...], wa_ref[...]) + _dot(gm_scr[...], wg_ref[...])
    o_ref[...] = x_ref[...] + gate_ref[0] * out


def _ag_out(attn, u, gv, mix, bias, wa, wg, x, gate, *, blocks_per_group):
    n, d = x.shape
    w = attn.shape[1]
    tm = ROW_BLOCK
    n_groups, chunk, _ = mix.shape
    row = lambda i: (i, 0)
    const2 = lambda i: (0, 0)
    return pl.pallas_call(
        functools.partial(_ag_out_kernel, n_groups=n_groups, chunk=chunk),
        grid=(n // tm,),
        in_specs=[pl.BlockSpec((tm, w), row), pl.BlockSpec((tm, w), row), pl.BlockSpec((tm, w), row),
                  pl.BlockSpec(mix.shape, lambda i: (0, 0, 0)),
                  pl.BlockSpec((tm, w), const2),
                  pl.BlockSpec((w, d), const2), pl.BlockSpec((w, d), const2),
                  pl.BlockSpec((tm, d), row),
                  _mod_spec(gate, blocks_per_group)],
        out_specs=pl.BlockSpec((tm, d), row),
        out_shape=jax.ShapeDtypeStruct((n, d), F32),
        scratch_shapes=[pltpu.VMEM((tm, w), BF16)],
        compiler_params=_params("arbitrary"),
        name="ag_out",
    )(attn, u, gv, mix, bias, wa, wg, x, gate)


def _conv_in_kernel(x_ref, g_ref, sh_ref, sc_ref, w_ref, bg_ref, p_ref, h_scr):
    @pl.when(pl.program_id(1) == 0)
    def _():
        h_scr[...] = _modnorm(x_ref[...], g_ref[...], sh_ref[0], sc_ref[0]).astype(BF16)

    tn = bg_ref.shape[1]
    z3 = _dot(h_scr[...], w_ref[...])
    bg_ref[...] = z3[:, 0:tn].astype(bg_ref.dtype)
    p_ref[...] = z3[:, tn:2 * tn] * z3[:, 2 * tn:3 * tn]


def _conv_in(x, g, shift, scale, w3t, *, blocks_per_group):
    n, d = x.shape
    tm, tn = ROW_BLOCK, CONV_COL_BLOCK
    row = lambda i, j: (i, 0)
    return pl.pallas_call(
        _conv_in_kernel,
        grid=(n // tm, d // tn),
        in_specs=[pl.BlockSpec((tm, d), row),
                  pl.BlockSpec((1, d), lambda i, j: (0, 0)),
                  _mod_spec(shift, blocks_per_group),
                  _mod_spec(scale, blocks_per_group),
                  pl.BlockSpec((d, 3 * tn), lambda i, j: (0, j))],
        out_specs=[pl.BlockSpec((tm, tn), lambda i, j: (i, j))] * 2,
        out_shape=(jax.ShapeDtypeStruct((n, d), BF16), jax.ShapeDtypeStruct((n, d), F32)),
        scratch_shapes=[pltpu.VMEM((tm, d), BF16)],
        compiler_params=_params("arbitrary", "arbitrary"),
        name="conv_in",
    )(x, g, shift, scale, w3t)


def _conv_out_kernel(p_ref, s1_ref, s2_ref, bg_ref, cw_ref, cb_ref, wo_ref, x_ref, gate_ref, o_ref, *, period):
    p = p_ref[...]
    t = lax.broadcasted_iota(jnp.int32, p.shape, 0) % period
    if period == p.shape[0]:
        s1 = jnp.broadcast_to(s1_ref[0, 0:1, :], p.shape)
        s2 = jnp.where(t == 0, jnp.broadcast_to(s2_ref[0, 0:1, :], p.shape),
                       jnp.broadcast_to(s2_ref[0, 1:2, :], p.shape))
    else:
        s1, s2 = s1_ref[...], s2_ref[...]
    p1 = jnp.where(t == 0, s1, pltpu.roll(p, 1, axis=0))
    p2 = jnp.where(t < 2, s2, pltpu.roll(p, 2, axis=0))
    y = cb_ref[...] + cw_ref[0:1, :] * p2
    y = y + cw_ref[1:2, :] * p1
    y = y + cw_ref[2:3, :] * p
    out = _dot((bg_ref[...].astype(F32) * y).astype(BF16), wo_ref[...])
    o_ref[...] = x_ref[...] + gate_ref[0] * out


def _conv_out(p, s1, s2, bg, cw, cb, wo, x, gate, *, blocks_per_group, period):
    n, d = x.shape
    tm = CONV_ROW_BLOCK
    row = lambda i: (i, 0)
    const2 = lambda i: (0, 0)
    if period == tm:
        s_spec = pl.BlockSpec((1, SUBLANES, d), lambda i: (i, 0, 0))
    else:
        s_spec = pl.BlockSpec((tm, d), row)
    return pl.pallas_call(
        functools.partial(_conv_out_kernel, period=period),
        grid=(n // tm,),
        in_specs=[pl.BlockSpec((tm, d), row), s_spec, s_spec, pl.BlockSpec((tm, d), row),
                  pl.BlockSpec(cw.shape, const2), pl.BlockSpec((1, d), const2),
                  pl.BlockSpec((d, d), const2), pl.BlockSpec((tm, d), row),
                  _mod_spec(gate, blocks_per_group)],
        out_specs=pl.BlockSpec((tm, d), row),
        out_shape=jax.ShapeDtypeStruct((n, d), F32),
        compiler_params=_params("arbitrary"),
        name="conv_out",
    )(p, s1, s2, bg, cw, cb, wo, x, gate)


def _batcher_pairs(n):
    pairs = []
    p = 1
    while p < n:
        k = p
        while k >= 1:
            for j in range(k % p, n - k, 2 * k):
                for i in range(min(k, n - j - k)):
                    if (i + j) // (2 * p) == (i + j + k) // (2 * p):
                        pairs.append((i + j, i + j + k))
            k //= 2
        p *= 2
    return pairs


def _exchange(v, i, j):
    v[i], v[j] = jnp.maximum(v[i], v[j]), jnp.minimum(v[i], v[j])


def _merge_top(a, b):
    n = len(a)
    c = [jnp.maximum(a[k], b[n - 1 - k]) for k in range(n)]
    d = n // 2
    while d >= 1:
        for i in range(n):
            if i & d == 0:
                _exchange(c, i, i + d)
        d //= 2
    return c


def _merge_sublanes(v):
    sh = SUBLANES // 2
    while sh >= 1:
        v = _merge_top(v, [pltpu.roll(x, sh, axis=0) for x in v])
        sh //= 2
    return v


def _top_sorted(s_ref):
    v = [s_ref[k * SUBLANES:(k + 1) * SUBLANES, :] for k in range(N_KEYS // SUBLANES)]
    for i, j in _batcher_pairs(len(v)):
        _exchange(v, i, j)
    return _merge_sublanes(v)


def _route_kernel(x_ref, g_ref, sh_ref, sc_ref, wqt_ref, k1_ref, k2_ref,
                  xt_ref, e1_ref, e2_ref, th_ref, qt_scr, s1_scr, s2_scr):
    assert N_KEYS // SUBLANES == PEER_TOPK
    tm = x_ref.shape[0]
    ht = _modnorm(x_ref[...], g_ref[...], sh_ref[0], sc_ref[0]).T.astype(BF16)
    xt_ref[...] = ht
    qt_scr[...] = _dot(wqt_ref[...], ht)
    d_key = 2 * N_KEYS
    sub = lax.broadcasted_iota(jnp.int32, (SUBLANES, tm), 0)

    def pack_rows(vals):
        out = vals[0]
        for s in range(1, SUBLANES):
            out = jnp.where(sub == s, vals[s], out)
        return out

    def head_body(hd, _):
        base = pl.multiple_of(hd * d_key, d_key)
        s1_scr[...] = _dot(k1_ref[...], qt_scr[pl.ds(base, N_KEYS), :].astype(BF16))
        s2_scr[...] = _dot(k2_ref[...], qt_scr[pl.ds(base + N_KEYS, N_KEYS), :].astype(BF16))
        t1 = _top_sorted(s1_scr)
        t2 = _top_sorted(s2_scr)
        a_lo = pack_rows(t1[:SUBLANES])
        a_hi = pack_rows(t1[SUBLANES:])
        lo = [a_lo + t for t in t2]
        hi = [a_hi + t for t in t2]
        top = _merge_sublanes(_merge_top(lo, hi))
        g16 = top[PEER_TOPK - 1]
        g17 = jnp.full((SUBLANES, tm), -jnp.inf, F32)
        for c in lo + hi:
            g17 = jnp.maximum(g17, jnp.where(c < g16, c, -jnp.inf))
        sh = SUBLANES // 2
        while sh >= 1:
            g17 = jnp.maximum(g17, pltpu.roll(g17, sh, axis=0))
            sh //= 2
        z = jnp.zeros((SUBLANES, tm), F32)
        for t in top:
            z = z + jnp.exp(t - top[0])
        row = lambda a: a[0:1, :]
        tau = 0.5 * (row(g16) + row(g17))
        s1 = s1_scr[...]
        s2 = s2_scr[...]
        e1_ref[hd] = jnp.where(s1 >= row(t1[PEER_TOPK - 1]), jnp.exp(s1 - row(t1[0])) * (0.5 / row(z)), 0.0)
        e2_ref[hd] = jnp.where(s2 >= row(t2[PEER_TOPK - 1]), jnp.exp(s2 - row(t2[0])), 0.0)
        th_ref[hd] = jnp.exp((tau - row(t2[0])) - s1)
        return 0

    lax.fori_loop(0, PEER_HEADS, head_body, 0)


def _route(x, g, shift, scale, wqt, k1, k2, *, blocks_per_group):
    n, d = x.shape
    tm = ROUTE_BLOCK
    qw = wqt.shape[0]
    row = lambda i: (i, 0)
    const2 = lambda i: (0, 0)
    fac = jax.ShapeDtypeStruct((PEER_HEADS, N_KEYS, n), F32)
    fac_spec = pl.BlockSpec((PEER_HEADS, N_KEYS, tm), lambda i: (0, 0, i))
    key_scr = pltpu.VMEM((N_KEYS, tm), F32)
    return pl.pallas_call(
        _route_kernel,
        grid=(n // tm,),
        in_specs=[pl.BlockSpec((tm, d), row), pl.BlockSpec((1, d), const2),
                  _mod_spec(shift, blocks_per_group * (ROW_BLOCK // tm)),
                  _mod_spec(scale, blocks_per_group * (ROW_BLOCK // tm)),
                  pl.BlockSpec((qw, d), const2),
                  pl.BlockSpec((N_KEYS, N_KEYS), const2), pl.BlockSpec((N_KEYS, N_KEYS), const2)],
        out_specs=[pl.BlockSpec((d, tm), lambda i: (0, i)), fac_spec, fac_spec, fac_spec],
        out_shape=(jax.ShapeDtypeStruct((d, n), BF16), fac, fac, fac),
        scratch_shapes=[pltpu.VMEM((qw, tm), F32), key_scr, key_scr],
        compiler_params=_params("arbitrary"),
        name="peer_route",
    )(x, g, shift, scale, wqt, k1, k2)


def _gelu_twice(a):
    c = 0.7978845608028654
    return a * (1.0 + jnp.tanh(a * (c + (c * 0.044715) * (a * a))))


def _peer_gate_rows(a_scr, h_out, e1_ref, e2_ref, th_ref, i_glob, r0):
    tm = a_scr.shape[1]
    key_rows = [(th_ref[hd, pl.ds(i_glob, 1), :], e1_ref[hd, pl.ds(i_glob, 1), :]) for hd in range(PEER_HEADS)]
    half = N_KEYS // 2
    for k in range(tm // LANES):
        cols = slice(k * LANES, (k + 1) * LANES)
        for j0 in (0, half):
            gate = None
            for hd in range(PEER_HEADS):
                e2 = e2_ref[hd, j0:j0 + half, cols]
                th, e1 = key_rows[hd]
                term = jnp.where(e2 >= th[:, cols], e2, 0.0) * e1[:, cols]
                gate = term if gate is None else gate + term
            rows = pl.ds(pl.multiple_of(r0 + j0, half), half)
            h_out[rows, cols] = (_gelu_twice(a_scr[rows, cols]) * gate).astype(BF16)


def _peer_stage(xt_ref, u_ref, v_ref, yt_ref, a_scr, h_in, h_out, facs, chunk):
    te = a_scr.shape[0]
    n_it = te // N_KEYS
    y_rows = yt_ref.shape[0] // n_it
    a_scr[...] = _dot(u_ref[...], xt_ref[...])

    def body(k, _):
        y0 = pl.multiple_of(k * y_rows, y_rows)
        yt_ref[pl.ds(y0, y_rows), :] += _dot(v_ref[pl.ds(y0, y_rows), :], h_in[...])
        _peer_gate_rows(a_scr, h_out, *facs, chunk * n_it + k, pl.multiple_of(k * N_KEYS, N_KEYS))
        return 0

    lax.fori_loop(0, n_it, body, 0)


def _peer_kernel(xt_ref, ua_ref, ub_ref, vp_ref, va_ref, e1_ref, e2_ref, th_ref, yt_ref,
                 a_scr, h0_scr, h1_scr, *, n_chunks):
    c = pl.program_id(1)
    facs = (e1_ref, e2_ref, th_ref)

    @pl.when(c == 0)
    def _():
        yt_ref[...] = jnp.zeros(yt_ref.shape, F32)
        h1_scr[...] = jnp.zeros(h1_scr.shape, BF16)

    @pl.when(c < n_chunks // 2)
    def _():
        _peer_stage(xt_ref, ua_ref, vp_ref, yt_ref, a_scr, h1_scr, h0_scr, facs, 2 * c)
        _peer_stage(xt_ref, ub_ref, va_ref, yt_ref, a_scr, h0_scr, h1_scr, facs, 2 * c + 1)

    @pl.when(c == n_chunks // 2)
    def _():
        yt_ref[...] += _dot(vp_ref[...], h1_scr[...])


def _peer(xt, u, vt, e1, e2, th):
    d, n = xt.shape
    ne = u.shape[0]
    tm, te = ROW_BLOCK, EXPERT_BLOCK
    n_chunks = ne // te
    assert n_chunks % 2 == 0
    last = n_chunks - 1
    fac_spec = pl.BlockSpec((PEER_HEADS, N_KEYS, tm), lambda i, c: (0, 0, i))
    u_spec = lambda f: pl.BlockSpec((te, d), lambda i, c: (f(c), 0))
    v_spec = lambda f: pl.BlockSpec((d, te), lambda i, c: (0, f(c)))
    return pl.pallas_call(
        functools.partial(_peer_kernel, n_chunks=n_chunks),
        grid=(n // tm, n_chunks // 2 + 1),
        in_specs=[pl.BlockSpec((d, tm), lambda i, c: (0, i)),
                  u_spec(lambda c: jnp.minimum(2 * c, last)),
                  u_spec(lambda c: jnp.minimum(2 * c + 1, last)),
                  v_spec(lambda c: jnp.maximum(2 * c - 1, 0)),
                  v_spec(lambda c: jnp.minimum(2 * c, last)),
                  fac_spec, fac_spec, fac_spec],
        out_specs=pl.BlockSpec((d, tm), lambda i, c: (0, i)),
        out_shape=jax.ShapeDtypeStruct((d, n), F32),
        scratch_shapes=[pltpu.VMEM((te, tm), F32), pltpu.VMEM((te, tm), BF16), pltpu.VMEM((te, tm), BF16)],
        compiler_params=_params("arbitrary", "arbitrary"),
        name="peer_dense",
    )(xt, u, u, vt, vt, e1, e2, th)


def _resid_kernel(x_ref, yt_ref, gate_ref, o_ref):
    o_ref[...] = x_ref[...] + gate_ref[0] * yt_ref[...].T


def _resid(x, yt, gate, *, blocks_per_group):
    n, d = x.shape
    tm = ROW_BLOCK
    row = lambda i: (i, 0)
    return pl.pallas_call(
        _resid_kernel,
        grid=(n // tm,),
        in_specs=[pl.BlockSpec((tm, d), row), pl.BlockSpec((d, tm), lambda i: (0, i)),
                  _mod_spec(gate, blocks_per_group)],
        out_specs=pl.BlockSpec((tm, d), row),
        out_shape=jax.ShapeDtypeStruct((n, d), F32),
        compiler_params=_params("arbitrary"),
        name="peer_resid",
    )(x, yt, gate)


class _Stream:
    def __init__(self, x, seq):
        self.batch = x.shape[0]
        self.seq = seq
        self.x = x.reshape(-1, x.shape[-1])
        self.long = seq % ROW_BLOCK == 0
        self.blocks_per_group = seq // ROW_BLOCK if self.long else 1

    def mod(self, m, k, tm=ROW_BLOCK):
        mk = m[:, k, :]
        if self.long:
            return mk[:, None, :]
        d = mk.shape[-1]
        return jnp.repeat(mk, self.seq, axis=0).reshape(-1, tm, d)


def kernel(x_prompt, x_sample, c_prompt, c_sample, cache_k, cache_v, cache_logf, page_table, state_conv,
           w_ada, b_ada, norm_mix, norm_ffn, w_in_ag, b_fgate, q_norm, k_norm, w_spatial, b_spatial,
           w_out_ag, w_in_conv, conv_w, conv_b, w_out_conv, peer_wq, peer_keys, peer_u, peer_v):
    bp, tp, d = x_prompt.shape
    bs, ts, _ = x_sample.shape
    depth = w_ada.shape[0]
    att_w = d // 2
    n_heads = att_w // HEAD_DIM
    n_groups = att_w // LANES
    assert tp % ROW_BLOCK == 0 and (bs * ts) % ROW_BLOCK == 0 and ROW_BLOCK % ts == 0 and ts == SUBLANES

    streams = [_Stream(x_prompt, tp), _Stream(x_sample, ts)]
    mod_all = _ada(jnp.concatenate([c_prompt, c_sample], axis=0), w_ada, b_ada)
    ck, cv = cache_k, cache_v
    clf = cache_logf.reshape(cache_logf.shape[0], cache_logf.shape[1], 1, PAGE_SIZE * n_heads)

    outs = {"k": [[], []], "v": [[], []], "f": [[], []], "g": [], "c": [[], []]}
    for layer in range(depth):
        ml = mod_all[layer].reshape(bp + bs, N_MOD, d)
        mods = [ml[:bp], ml[bp:]]
        g_mix = norm_mix[layer][None, :]
        g_ffn = norm_ffn[layer][None, :]
        if layer % 2 == 0:
            a = layer // 2
            w_in = w_in_ag[a]
            w5 = jnp.concatenate([w_in[:, :3 * att_w], w_in[:, 3 * att_w + n_heads:]], axis=1).astype(BF16)
            wf = jnp.pad(w_in[:, 3 * att_w:3 * att_w + n_heads], ((0, 0), (0, LANES - n_heads))).astype(BF16)
            bf = jnp.pad(b_fgate[a], (0, LANES - n_heads))[None, :]
            wa = w_out_ag[a][:att_w].astype(BF16)
            wg = w_out_ag[a][att_w:].astype(BF16)
            tril = jnp.tril(w_spatial[a])
        else:
            ci = layer // 2
            nj = d // CONV_COL_BLOCK
            w3 = w_in_conv[ci].reshape(d, 3, nj, CONV_COL_BLOCK).transpose(0, 2, 1, 3).reshape(d, 3 * d).astype(BF16)
            wo = w_out_conv[ci].astype(BF16)
        wqt = peer_wq[layer].T.astype(BF16)
        k1 = peer_keys[layer, 0].astype(BF16)
        k2 = peer_keys[layer, 1].astype(BF16)
        pu = peer_u[layer].astype(BF16)
        pvt = peer_v[layer].T.astype(BF16)

        for si, st in enumerate(streams):
            m = mods[si]
            bpg = st.blocks_per_group
            if layer % 2 == 0:
                q, k, v, u, gv, lf = _ag_in(st.x, g_mix, st.mod(m, 0), st.mod(m, 1), w5, wf, bf,
                                            q_norm[a][None, :], k_norm[a][None, :],
                                            blocks_per_group=bpg, q_dtype=BF16 if st.long else F32)
                if st.long:
                    f, ft = _cumsum(lf.reshape(st.batch, st.seq, LANES))
                    tk = min(ATTN_K_BLOCK, st.seq)
                    ft4 = ft[:, :n_heads, :].reshape(st.batch, n_heads, st.seq // tk, tk)
                    attn = _attn_prompt(q.reshape(st.batch, st.seq, att_w), k.reshape(st.batch, st.seq, att_w),
                                        v.reshape(st.batch, st.seq, att_w), f, ft4)
                    mix = tril.astype(BF16)
                    bias = jnp.tile(jnp.repeat(b_spatial[a].T, LANES, axis=1), (ROW_BLOCK // CHUNK, 1))
                else:
                    nr = st.seq * n_heads
                    lfn = jnp.pad(lf[:, :n_heads].reshape(st.batch, 1, nr), ((0, 0), (0, 0), (0, LANES - nr)))
                    attn = _attn_sample(page_table, q.reshape(st.batch, nr, HEAD_DIM),
                                        k.reshape(st.batch, nr, HEAD_DIM), v.reshape(st.batch, nr, HEAD_DIM),
                                        lfn, ck, cv, clf, a)
                    eye = jnp.eye(ROW_BLOCK // st.seq, dtype=F32)
                    mix = jax.vmap(lambda t: jnp.kron(eye, t))(tril[:, :st.seq, :st.seq]).astype(BF16)
                    bias = jnp.tile(jnp.repeat(b_spatial[a].T[:st.seq], LANES, axis=1), (ROW_BLOCK // st.seq, 1))
                    outs["g"].append(gv.reshape(st.batch, st.seq, att_w))
                st.x = _ag_out(attn.reshape(-1, att_w), u, gv, mix, bias, wa, wg, st.x, st.mod(m, 2),
                               blocks_per_group=bpg)
                outs["k"][si].append(k.reshape(st.batch, st.seq, n_heads, HEAD_DIM))
                outs["v"][si].append(v.reshape(st.batch, st.seq, n_heads, HEAD_DIM))
                outs["f"][si].append(lf[:, :n_heads].reshape(st.batch, st.seq, n_heads))
            else:
                bg, p = _conv_in(st.x, g_mix, st.mod(m, 0), st.mod(m, 1), w3, blocks_per_group=bpg)
                p3 = p.reshape(st.batch, st.seq, d)
                tc = CONV_ROW_BLOCK
                if st.long:
                    cb = st.seq // tc
                    pb = p.reshape(st.batch, cb, tc, d)[:, :, tc - SUBLANES:, :]
                    prev = jnp.concatenate([jnp.zeros_like(pb[:, :1]), pb[:, :-1]], axis=1)
                    prev = prev.reshape(st.batch * cb, SUBLANES, d)
                    s2 = prev[:, SUBLANES - 2:, :]
                    s2 = jnp.pad(s2, ((0, 0), (0, SUBLANES - 2), (0, 0)))
                    s1 = jnp.pad(prev[:, SUBLANES - 1:, :], ((0, 0), (0, SUBLANES - 1), (0, 0)))
                    period = tc
                else:
                    cb = 1
                    buf = state_conv[ci]
                    s2 = jnp.pad(buf, ((0, 0), (0, st.seq - 2), (0, 0))).reshape(-1, d)
                    s1 = jnp.pad(buf[:, 1:], ((0, 0), (0, st.seq - 1), (0, 0))).reshape(-1, d)
                    period = st.seq
                st.x = _conv_out(p, s1, s2, bg, conv_w[ci], conv_b[ci][None, :], wo, st.x, st.mod(m, 2, tc),
                                 blocks_per_group=cb, period=period)
                outs["c"][si].append(p3[:, st.seq - 2:, :])
            xt, e1, e2, th = _route(st.x, g_ffn, st.mod(m, 3, ROUTE_BLOCK), st.mod(m, 4, ROUTE_BLOCK),
                                    wqt, k1, k2, blocks_per_group=bpg)
            yt = _peer(xt, pu, pvt, e1, e2, th)
            st.x = _resid(st.x, yt, st.mod(m, 5), blocks_per_group=bpg)

    sp, ss = streams
    return (sp.x.reshape(bp, tp, d), ss.x.reshape(bs, ts, d),
            jnp.stack(outs["k"][0]), jnp.stack(outs["v"][0]), jnp.stack(outs["f"][0]),
            jnp.stack(outs["k"][1]), jnp.stack(outs["v"][1]), jnp.stack(outs["f"][1]),
            jnp.stack(outs["g"]), jnp.stack(outs["c"][0]), jnp.stack(outs["c"][1]))
```

```python
import functools

import jax
import jax.numpy as jnp
from jax import lax
from jax.experimental import pallas as pl
from jax.experimental.pallas import tpu as pltpu

F32 = jnp.float32
BF16 = jnp.bfloat16

EPS = 1e-6
LANES = 128
SUBLANES = 8
BF16_SUBLANES = 16
HEAD_DIM = 128
N_MOD = 6
PEER_HEADS = 8
PEER_TOPK = 16
N_KEYS = 128
PAGE_SIZE = 128
CHUNK = 128
VMEM_LIMIT_BYTES = 56 * 1024 * 1024
NEG_BIG = -1e30
ROW_BLOCK = 512
ROUTE_BLOCK = 256
EXPERT_BLOCK = 1024
PAGES_PER_STEP = 16
ATTN_K_BLOCK = 512
ADA_COL_BLOCK = 1024
CONV_COL_BLOCK = 512
CONV_ROW_BLOCK = 256


def _params(*sem):
    return pltpu.CompilerParams(dimension_semantics=sem, vmem_limit_bytes=VMEM_LIMIT_BYTES)


def _modnorm(x, g, shift, scale):
    y = x * lax.rsqrt(jnp.mean(x * x, axis=-1, keepdims=True) + EPS)
    return (y * g) * (1.0 + scale) + shift


def _gelu(a):
    return jax.nn.gelu(a)


def _log_sigmoid(x):
    return jnp.minimum(x, 0.0) - jnp.log1p(jnp.exp(-jnp.abs(x)))


def _split3(x):
    hi = x.astype(BF16)
    r = x - hi.astype(F32)
    mid = r.astype(BF16)
    lo = (r - mid.astype(F32)).astype(BF16)
    return hi, mid, lo


def _dot(a, b):
    return jnp.dot(a, b, preferred_element_type=F32)


def _dot_nt(a, b):
    return lax.dot_general(a, b, (((1,), (1,)), ((), ())), preferred_element_type=F32)


def _mod_spec(mod, blocks_per_group):
    _, r, d = mod.shape
    if r == 1:
        return pl.BlockSpec((1, 1, d), lambda i, *_: (i // blocks_per_group, 0, 0))
    return pl.BlockSpec((1, r, d), lambda i, *_: (i, 0, 0))


def _ada_kernel(c_ref, w_ref, b_ref, o_ref):
    c = c_ref[...]
    s = (c * jax.nn.sigmoid(c)).astype(BF16)
    o_ref[0] = _dot(s, w_ref[0].astype(BF16)) + b_ref[0]


def _ada(c_all, w_ada, b_ada):
    depth, d, n6 = w_ada.shape
    r = c_all.shape[0]
    tn = ADA_COL_BLOCK
    return pl.pallas_call(
        _ada_kernel,
        grid=(depth, n6 // tn),
        in_specs=[pl.BlockSpec((r, d), lambda l, j: (0, 0)),
                  pl.BlockSpec((1, d, tn), lambda l, j: (l, 0, j)),
                  pl.BlockSpec((1, 1, tn), lambda l, j: (l, 0, j))],
        out_specs=pl.BlockSpec((1, r, tn), lambda l, j: (l, 0, j)),
        out_shape=jax.ShapeDtypeStruct((depth, r, n6), F32),
        compiler_params=_params("arbitrary", "arbitrary"),
        name="ada_mod",
    )(c_all, w_ada, b_ada.reshape(depth, 1, n6))


def _ag_in_kernel(x_ref, g_ref, sh_ref, sc_ref, w_ref, wf_ref, bf_ref, qg_ref, kg_ref,
                  q_ref, k_ref, v_ref, u_ref, gv_ref, lf_ref, h_scr, *, n_heads):
    j = pl.program_id(1)

    @pl.when(j == 0)
    def _():
        h = _modnorm(x_ref[...], g_ref[...], sh_ref[0], sc_ref[0]).astype(BF16)
        h_scr[...] = h
        lf_ref[...] = _log_sigmoid(_dot(h, wf_ref[...]) + bf_ref[...])

    def z():
        return _dot(h_scr[...], w_ref[...])

    def head_norm(zz, g, o_ref, post):
        for hd in range(n_heads):
            zh = zz[:, hd * HEAD_DIM:(hd + 1) * HEAD_DIM]
            y = zh * lax.rsqrt(jnp.mean(zh * zh, axis=-1, keepdims=True) + EPS) * g
            o_ref[:, hd * HEAD_DIM:(hd + 1) * HEAD_DIM] = (y * post).astype(o_ref.dtype)

    @pl.when(j == 0)
    def _():
        head_norm(z(), qg_ref[...], q_ref, HEAD_DIM ** -0.5)

    @pl.when(j == 1)
    def _():
        head_norm(z(), kg_ref[...], k_ref, 1.0)

    @pl.when(j == 2)
    def _():
        v_ref[...] = z()

    @pl.when(j == 3)
    def _():
        u_ref[...] = _gelu(z()).astype(u_ref.dtype)

    @pl.when(j == 4)
    def _():
        gv_ref[...] = _gelu(z())


def _ag_in(x, g, shift, scale, w5, wf, bf, qg, kg, *, blocks_per_group, q_dtype):
    n, d = x.shape
    w = w5.shape[1] // 5
    tm = ROW_BLOCK
    row = lambda i, j: (i, 0)
    const = lambda i, j: (0, 0)
    out_shapes = (jax.ShapeDtypeStruct((n, w), q_dtype),
                  jax.ShapeDtypeStruct((n, w), F32),
                  jax.ShapeDtypeStruct((n, w), F32),
                  jax.ShapeDtypeStruct((n, w), BF16),
                  jax.ShapeDtypeStruct((n, w), F32),
                  jax.ShapeDtypeStruct((n, LANES), F32))
    return pl.pallas_call(
        functools.partial(_ag_in_kernel, n_heads=w // HEAD_DIM),
        grid=(n // tm, 5),
        in_specs=[pl.BlockSpec((tm, d), row),
                  pl.BlockSpec((1, d), const),
                  _mod_spec(shift, blocks_per_group),
                  _mod_spec(scale, blocks_per_group),
                  pl.BlockSpec((d, w), lambda i, j: (0, j)),
                  pl.BlockSpec((d, LANES), const),
                  pl.BlockSpec((1, LANES), const),
                  pl.BlockSpec((1, HEAD_DIM), const),
                  pl.BlockSpec((1, HEAD_DIM), const)],
        out_specs=[pl.BlockSpec((tm, w), row)] * 5 + [pl.BlockSpec((tm, LANES), row)],
        out_shape=out_shapes,
        scratch_shapes=[pltpu.VMEM((tm, d), BF16)],
        compiler_params=_params("arbitrary", "arbitrary"),
        name="ag_in",
    )(x, g, shift, scale, w5, wf, bf, qg, kg)


def _tri_ones(n):
    r = lax.broadcasted_iota(jnp.int32, (n, n), 0)
    c = lax.broadcasted_iota(jnp.int32, (n, n), 1)
    return jnp.where(r >= c, 1.0, 0.0).astype(BF16)


def _cumsum_kernel(lf_ref, f_ref, ft_ref):
    t = lf_ref.shape[1]
    tri = _tri_ones(CHUNK)
    carry = jnp.zeros((1, LANES), F32)
    for c in range(t // CHUNK):
        hi, mid, lo = _split3(lf_ref[0, c * CHUNK:(c + 1) * CHUNK, :])
        cs = (_dot(tri, hi) + _dot(tri, mid)) + _dot(tri, lo) + carry
        f_ref[0, c * CHUNK:(c + 1) * CHUNK, :] = cs
        carry = cs[CHUNK - 1:CHUNK, :]
    ft_ref[0] = f_ref[0].T


def _cumsum(lf):
    b, t, _ = lf.shape
    return pl.pallas_call(
        _cumsum_kernel,
        grid=(b,),
        in_specs=[pl.BlockSpec((1, t, LANES), lambda i: (i, 0, 0))],
        out_specs=[pl.BlockSpec((1, t, LANES), lambda i: (i, 0, 0)),
                   pl.BlockSpec((1, LANES, t), lambda i: (i, 0, 0))],
        out_shape=(jax.ShapeDtypeStruct((b, t, LANES), F32), jax.ShapeDtypeStruct((b, LANES, t), F32)),
        compiler_params=_params("arbitrary"),
        name="fox_cumsum",
    )(lf)


def _attn_kernel(q_ref, k_ref, v_ref, f_ref, ft_ref, o_ref, *, tb):
    hd = pl.program_id(1)
    t = q_ref.shape[1]
    nb = t // tb
    lane = lax.broadcasted_iota(jnp.int32, (tb, LANES), 1)
    row = lax.broadcasted_iota(jnp.int32, (tb, tb), 0)
    col = lax.broadcasted_iota(jnp.int32, (tb, tb), 1)
    kb = [k_ref[0, j * tb:(j + 1) * tb, :].astype(BF16) for j in range(nb)]
    vb = [v_ref[0, j * tb:(j + 1) * tb, :].astype(BF16) for j in range(nb)]
    fk = [ft_ref[0, hd, j:j + 1, :] for j in range(nb)]
    for i in range(nb):
        rows = slice(i * tb, (i + 1) * tb)
        q = q_ref[0, rows, :]
        fq = jnp.sum(jnp.where(lane == hd, f_ref[0, rows, :], 0.0), axis=-1, keepdims=True)
        logits = []
        for j in range(i + 1):
            s = _dot_nt(q, kb[j]) + fq - fk[j]
            logits.append(jnp.where(col <= row, s, NEG_BIG) if j == i else s)
        m = jnp.max(logits[0], axis=-1, keepdims=True)
        for s in logits[1:]:
            m = jnp.maximum(m, jnp.max(s, axis=-1, keepdims=True))
        l = jnp.zeros((tb, 1), F32)
        acc = jnp.zeros((tb, HEAD_DIM), F32)
        for j, s in enumerate(logits):
            p = jnp.exp(s - m)
            l = l + jnp.sum(p, axis=-1, keepdims=True)
            acc = acc + _dot(p.astype(BF16), vb[j])
        o_ref[0, rows, :] = (acc / l).astype(o_ref.dtype)


def _attn_prompt(q, k, v, f, ft4):
    b, t, w = k.shape
    n_heads = w // HEAD_DIM
    tb = ft4.shape[3]
    seq_spec = lambda last, col: pl.BlockSpec((1, t, last), lambda bi, h: (bi, 0, h if col else 0))
    return pl.pallas_call(
        functools.partial(_attn_kernel, tb=tb),
        grid=(b, n_heads),
        in_specs=[seq_spec(HEAD_DIM, True), seq_spec(HEAD_DIM, True), seq_spec(HEAD_DIM, True),
                  seq_spec(LANES, False),
                  pl.BlockSpec((1, n_heads, t // tb, tb), lambda bi, h: (bi, 0, 0, 0))],
        out_specs=seq_spec(HEAD_DIM, True),
        out_shape=jax.ShapeDtypeStruct((b, t, w), BF16),
        compiler_params=_params("arbitrary", "arbitrary"),
        name="fox_prompt",
    )(q, k, v, f, ft4)


def _attn_sample_kernel(pt_ref, q_ref, kn_ref, vn_ref, lfn_ref, suall_ref, cu_ref, *rest, n_heads, pages_per_step):
    npg = pages_per_step
    kp_refs = rest[:npg]
    vp_refs = rest[npg:2 * npg]
    lp_refs = rest[2 * npg:3 * npg]
    o_ref = rest[3 * npg]
    m_scr, l_scr, acc_scr, carry_scr, colb_scr, cflat_scr = rest[3 * npg + 1:]
    g = pl.program_id(1)
    n_g = pl.num_programs(1)
    nr = q_ref.shape[1]
    pk = PAGE_SIZE * n_heads
    q = q_ref[0].astype(BF16)

    def head_match(cols):
        r = lax.broadcasted_iota(jnp.int32, (nr, cols), 0)
        c = lax.broadcasted_iota(jnp.int32, (nr, cols), 1)
        return r, c, (c % n_heads) == (r % n_heads)

    def online(logits, values):
        m_old = m_scr[...]
        m_new = m_old
        for s in logits:
            m_new = jnp.maximum(m_new, jnp.max(s, axis=-1, keepdims=True))
        alpha = jnp.exp(m_old - m_new)
        l_new = alpha * l_scr[...]
        acc = alpha * acc_scr[...]
        for s, vb in zip(logits, values):
            p = jnp.exp(s - m_new)
            l_new = l_new + jnp.sum(p, axis=-1, keepdims=True)
            acc = acc + _dot(p.astype(BF16), vb)
        l_scr[...] = l_new
        acc_scr[...] = acc
        m_scr[...] = m_new

    def stack3(x):
        parts = [p.astype(F32) for p in _split3(x)]
        pad = -(3 * x.shape[0]) % BF16_SUBLANES
        if pad:
            parts.append(jnp.zeros((pad, x.shape[1]), F32))
        return jnp.concatenate(parts, axis=0).astype(BF16)

    @pl.when(g == 0)
    def _():
        m_scr[...] = jnp.full(m_scr.shape, NEG_BIG, F32)
        l_scr[...] = jnp.zeros(l_scr.shape, F32)
        acc_scr[...] = jnp.zeros(acc_scr.shape, F32)
        carry_scr[...] = jnp.zeros(carry_scr.shape, F32)
        cs = _dot(stack3(lfn_ref[0]), cu_ref[...])
        cflat = cs[0:1, :] + cs[1:2, :] + cs[2:3, :]
        cflat_scr[...] = cflat
        r, c, _ = head_match(LANES)
        colb_scr[...] = jnp.sum(jnp.where(r == c, cflat, 0.0), axis=-1, keepdims=True)

    lf_rows = jnp.concatenate([lp_refs[r][0, 0] for r in range(npg)], axis=0)
    sums = _dot(stack3(lf_rows), suall_ref[...])
    sums = sums[0:npg, :] + sums[npg:2 * npg, :] + sums[2 * npg:3 * npg, :]
    colb = colb_scr[...]
    _, _, hm = head_match(pk)
    carry = carry_scr[...]
    logits, values = [], []
    for r in range(npg):
        bias = sums[r:r + 1, 0:pk] + carry
        carry = carry + sums[r:r + 1, pk:2 * pk]
        kb = kp_refs[r][0, 0].reshape(pk, HEAD_DIM).astype(BF16)
        logits.append(jnp.where(hm, _dot_nt(q, kb) + bias + colb, NEG_BIG))
        values.append(vp_refs[r][0, 0].reshape(pk, HEAD_DIM).astype(BF16))
    online(logits, values)
    carry_scr[...] = carry

    @pl.when(g == n_g - 1)
    def _():
        zeros = jnp.zeros((LANES - nr, HEAD_DIM), F32)
        kb = jnp.concatenate([kn_ref[0], zeros], axis=0).astype(BF16)
        vb = jnp.concatenate([vn_ref[0], zeros], axis=0).astype(BF16)
        r, c, hmn = head_match(LANES)
        ok = hmn & (c // n_heads <= r // n_heads) & (c < nr)
        online([jnp.where(ok, _dot_nt(q, kb) + colb - cflat_scr[...], NEG_BIG)], [vb])
        o_ref[0] = (acc_scr[...] / l_scr[...]).astype(o_ref.dtype)


def _attn_sample(page_table, q, kn, vn, lfn, cache_k, cache_v, cache_lf, layer):
    nb, nr, _ = q.shape
    n_heads = cache_k.shape[3]
    pk = PAGE_SIZE * n_heads
    n_pages = page_table.shape[1]
    npg = PAGES_PER_STEP
    while n_pages % npg:
        npg //= 2
    assert nr <= LANES

    idx = jnp.arange(pk)
    same = (idx[:, None] % n_heads) == (idx[None, :] % n_heads)
    later = (idx[:, None] // n_heads) > (idx[None, :] // n_heads)
    suall = jnp.concatenate([same & later, same], axis=1).astype(BF16)
    il = jnp.arange(LANES)
    cu = ((il[:, None] % n_heads == il[None, :] % n_heads) & (il[:, None] // n_heads <= il[None, :] // n_heads)
          & (il[:, None] < nr) & (il[None, :] < nr)).astype(BF16)

    def page_spec(r, last):
        def imap(b, g, pt):
            return (layer, pt[b, n_pages - 1 - (g * npg + r)]) + (0,) * len(last)
        return pl.BlockSpec((1, 1) + last, imap)

    new_spec = lambda rows, last: pl.BlockSpec((1, rows, last), lambda b, g, pt: (b, 0, 0))
    const = lambda shape: pl.BlockSpec(shape, lambda b, g, pt: (0, 0))
    in_specs = ([new_spec(nr, HEAD_DIM)] * 3 + [new_spec(1, LANES), const(suall.shape), const(cu.shape)]
                + [page_spec(r, (PAGE_SIZE, n_heads, HEAD_DIM)) for r in range(npg)]
                + [page_spec(r, (PAGE_SIZE, n_heads, HEAD_DIM)) for r in range(npg)]
                + [page_spec(r, (1, pk)) for r in range(npg)])
    grid_spec = pltpu.PrefetchScalarGridSpec(
        num_scalar_prefetch=1,
        grid=(nb, n_pages // npg),
        in_specs=in_specs,
        out_specs=pl.BlockSpec((1, nr, HEAD_DIM), lambda b, g, pt: (b, 0, 0)),
        scratch_shapes=[pltpu.VMEM((nr, 1), F32), pltpu.VMEM((nr, 1), F32), pltpu.VMEM((nr, HEAD_DIM), F32),
                        pltpu.VMEM((1, pk), F32), pltpu.VMEM((nr, 1), F32), pltpu.VMEM((1, LANES), F32)])
    return pl.pallas_call(
        functools.partial(_attn_sample_kernel, n_heads=n_heads, pages_per_step=npg),
        grid_spec=grid_spec,
        out_shape=jax.ShapeDtypeStruct((nb, nr, HEAD_DIM), BF16),
        compiler_params=_params("arbitrary", "arbitrary"),
        name="fox_sample",
    )(page_table, q, kn, vn, lfn, suall, cu, *([cache_k] * npg), *([cache_v] * npg), *([cache_lf] * npg))


def _ag_out_kernel(attn_ref, u_ref, gv_ref, mix_ref, bias_ref, wa_ref, wg_ref, x_ref, gate_ref,
                   o_ref, gm_scr, *, n_groups, chunk):
    tm = x_ref.shape[0]
    for g in range(n_groups):
        cols = slice(g * LANES, (g + 1) * LANES)
        for c in range(tm // chunk):
            rows = slice(c * chunk, (c + 1) * chunk)
            s = _dot(mix_ref[g], gv_ref[rows, cols].astype(BF16)) + bias_ref[rows, cols]
            gm_scr[rows, cols] = (u_ref[rows, cols].astype(F32) * s).astype(BF16)
    out = _dot(attn_ref[...], wa_ref[...]) + _dot(gm_scr[...], wg_ref[...])
    o_ref[...] = x_ref[...] + gate_ref[0] * out


def _ag_out(attn, u, gv, mix, bias, wa, wg, x, gate, *, blocks_per_group):
    n, d = x.shape
    w = attn.shape[1]
    tm = ROW_BLOCK
    n_groups, chunk, _ = mix.shape
    row = lambda i: (i, 0)
    const2 = lambda i: (0, 0)
    return pl.pallas_call(
        functools.partial(_ag_out_kernel, n_groups=n_groups, chunk=chunk),
        grid=(n // tm,),
        in_specs=[pl.BlockSpec((tm, w), row), pl.BlockSpec((tm, w), row), pl.BlockSpec((tm, w), row),
                  pl.BlockSpec(mix.shape, lambda i: (0, 0, 0)),
                  pl.BlockSpec((tm, w), const2),
                  pl.BlockSpec((w, d), const2), pl.BlockSpec((w, d), const2),
                  pl.BlockSpec((tm, d), row),
                  _mod_spec(gate, blocks_per_group)],
        out_specs=pl.BlockSpec((tm, d), row),
        out_shape=jax.ShapeDtypeStruct((n, d), F32),
        scratch_shapes=[pltpu.VMEM((tm, w), BF16)],
        compiler_params=_params("arbitrary"),
        name="ag_out",
    )(attn, u, gv, mix, bias, wa, wg, x, gate)


def _conv_in_kernel(x_ref, g_ref, sh_ref, sc_ref, w_ref, bg_ref, p_ref, h_scr):
    @pl.when(pl.program_id(1) == 0)
    def _():
        h_scr[...] = _modnorm(x_ref[...], g_ref[...], sh_ref[0], sc_ref[0]).astype(BF16)

    tn = bg_ref.shape[1]
    z3 = _dot(h_scr[...], w_ref[...])
    bg_ref[...] = z3[:, 0:tn].astype(bg_ref.dtype)
    p_ref[...] = z3[:, tn:2 * tn] * z3[:, 2 * tn:3 * tn]


def _conv_in(x, g, shift, scale, w3t, *, blocks_per_group):
    n, d = x.shape
    tm, tn = ROW_BLOCK, CONV_COL_BLOCK
    row = lambda i, j: (i, 0)
    return pl.pallas_call(
        _conv_in_kernel,
        grid=(n // tm, d // tn),
        in_specs=[pl.BlockSpec((tm, d), row),
                  pl.BlockSpec((1, d), lambda i, j: (0, 0)),
                  _mod_spec(shift, blocks_per_group),
                  _mod_spec(scale, blocks_per_group),
                  pl.BlockSpec((d, 3 * tn), lambda i, j: (0, j))],
        out_specs=[pl.BlockSpec((tm, tn), lambda i, j: (i, j))] * 2,
        out_shape=(jax.ShapeDtypeStruct((n, d), BF16), jax.ShapeDtypeStruct((n, d), F32)),
        scratch_shapes=[pltpu.VMEM((tm, d), BF16)],
        compiler_params=_params("arbitrary", "arbitrary"),
        name="conv_in",
    )(x, g, shift, scale, w3t)


def _conv_out_kernel(p_ref, s1_ref, s2_ref, bg_ref, cw_ref, cb_ref, wo_ref, x_ref, gate_ref, o_ref, *, period):
    p = p_ref[...]
    t = lax.broadcasted_iota(jnp.int32, p.shape, 0) % period
    if period == p.shape[0]:
        s1 = jnp.broadcast_to(s1_ref[0, 0:1, :], p.shape)
        s2 = jnp.where(t == 0, jnp.broadcast_to(s2_ref[0, 0:1, :], p.shape),
                       jnp.broadcast_to(s2_ref[0, 1:2, :], p.shape))
    else:
        s1, s2 = s1_ref[...], s2_ref[...]
    p1 = jnp.where(t == 0, s1, pltpu.roll(p, 1, axis=0))
    p2 = jnp.where(t < 2, s2, pltpu.roll(p, 2, axis=0))
    y = cb_ref[...] + cw_ref[0:1, :] * p2
    y = y + cw_ref[1:2, :] * p1
    y = y + cw_ref[2:3, :] * p
    out = _dot((bg_ref[...].astype(F32) * y).astype(BF16), wo_ref[...])
    o_ref[...] = x_ref[...] + gate_ref[0] * out


def _conv_out(p, s1, s2, bg, cw, cb, wo, x, gate, *, blocks_per_group, period):
    n, d = x.shape
    tm = CONV_ROW_BLOCK
    row = lambda i: (i, 0)
    const2 = lambda i: (0, 0)
    if period == tm:
        s_spec = pl.BlockSpec((1, SUBLANES, d), lambda i: (i, 0, 0))
    else:
        s_spec = pl.BlockSpec((tm, d), row)
    return pl.pallas_call(
        functools.partial(_conv_out_kernel, period=period),
        grid=(n // tm,),
        in_specs=[pl.BlockSpec((tm, d), row), s_spec, s_spec, pl.BlockSpec((tm, d), row),
                  pl.BlockSpec(cw.shape, const2), pl.BlockSpec((1, d), const2),
                  pl.BlockSpec((d, d), const2), pl.BlockSpec((tm, d), row),
                  _mod_spec(gate, blocks_per_group)],
        out_specs=pl.BlockSpec((tm, d), row),
        out_shape=jax.ShapeDtypeStruct((n, d), F32),
        compiler_params=_params("arbitrary"),
        name="conv_out",
    )(p, s1, s2, bg, cw, cb, wo, x, gate)


def _batcher_pairs(n):
    pairs = []
    p = 1
    while p < n:
        k = p
        while k >= 1:
            for j in range(k % p, n - k, 2 * k):
                for i in range(min(k, n - j - k)):
                    if (i + j) // (2 * p) == (i + j + k) // (2 * p):
                        pairs.append((i + j, i + j + k))
            k //= 2
        p *= 2
    return pairs


def _exchange(v, i, j):
    v[i], v[j] = jnp.maximum(v[i], v[j]), jnp.minimum(v[i], v[j])


def _merge_top(a, b):
    n = len(a)
    c = [jnp.maximum(a[k], b[n - 1 - k]) for k in range(n)]
    d = n // 2
    while d >= 1:
        for i in range(n):
            if i & d == 0:
                _exchange(c, i, i + d)
        d //= 2
    return c


def _merge_sublanes(v):
    sh = SUBLANES // 2
    while sh >= 1:
        v = _merge_top(v, [pltpu.roll(x, sh, axis=0) for x in v])
        sh //= 2
    return v


def _top_sorted(s_ref):
    v = [s_ref[k * SUBLANES:(k + 1) * SUBLANES, :] for k in range(N_KEYS // SUBLANES)]
    for i, j in _batcher_pairs(len(v)):
        _exchange(v, i, j)
    return _merge_sublanes(v)


def _route_kernel(x_ref, g_ref, sh_ref, sc_ref, wqt_ref, k1_ref, k2_ref,
                  xt_ref, e1_ref, e2_ref, th_ref, qt_scr, s1_scr, s2_scr):
    assert N_KEYS // SUBLANES == PEER_TOPK
    tm = x_ref.shape[0]
    ht = _modnorm(x_ref[...], g_ref[...], sh_ref[0], sc_ref[0]).T.astype(BF16)
    xt_ref[...] = ht
    qt_scr[...] = _dot(wqt_ref[...], ht)
    d_key = 2 * N_KEYS
    sub = lax.broadcasted_iota(jnp.int32, (SUBLANES, tm), 0)

    def pack_rows(vals):
        out = vals[0]
        for s in range(1, SUBLANES):
            out = jnp.where(sub == s, vals[s], out)
        return out

    def head_body(hd, _):
        base = pl.multiple_of(hd * d_key, d_key)
        s1_scr[...] = _dot(k1_ref[...], qt_scr[pl.ds(base, N_KEYS), :].astype(BF16))
        s2_scr[...] = _dot(k2_ref[...], qt_scr[pl.ds(base + N_KEYS, N_KEYS), :].astype(BF16))
        t1 = _top_sorted(s1_scr)
        t2 = _top_sorted(s2_scr)
        a_lo = pack_rows(t1[:SUBLANES])
        a_hi = pack_rows(t1[SUBLANES:])
        lo = [a_lo + t for t in t2]
        hi = [a_hi + t for t in t2]
        top = _merge_sublanes(_merge_top(lo, hi))
        g16 = top[PEER_TOPK - 1]
        g17 = jnp.full((SUBLANES, tm), -jnp.inf, F32)
        for c in lo + hi:
            g17 = jnp.maximum(g17, jnp.where(c < g16, c, -jnp.inf))
        sh = SUBLANES // 2
        while sh >= 1:
            g17 = jnp.maximum(g17, pltpu.roll(g17, sh, axis=0))
            sh //= 2
        z = jnp.zeros((SUBLANES, tm), F32)
        for t in top:
            z = z + jnp.exp(t - top[0])
        row = lambda a: a[0:1, :]
        tau = 0.5 * (row(g16) + row(g17))
        s1 = s1_scr[...]
        s2 = s2_scr[...]
        e1_ref[hd] = jnp.where(s1 >= row(t1[PEER_TOPK - 1]), jnp.exp(s1 - row(t1[0])) * (0.5 / row(z)), 0.0)
        e2_ref[hd] = jnp.where(s2 >= row(t2[PEER_TOPK - 1]), jnp.exp(s2 - row(t2[0])), 0.0)
        th_ref[hd] = jnp.exp((tau - row(t2[0])) - s1)
        return 0

    lax.fori_loop(0, PEER_HEADS, head_body, 0)


def _route(x, g, shift, scale, wqt, k1, k2, *, blocks_per_group):
    n, d = x.shape
    tm = ROUTE_BLOCK
    qw = wqt.shape[0]
    row = lambda i: (i, 0)
    const2 = lambda i: (0, 0)
    fac = jax.ShapeDtypeStruct((PEER_HEADS, N_KEYS, n), F32)
    fac_spec = pl.BlockSpec((PEER_HEADS, N_KEYS, tm), lambda i: (0, 0, i))
    key_scr = pltpu.VMEM((N_KEYS, tm), F32)
    return pl.pallas_call(
        _route_kernel,
        grid=(n // tm,),
        in_specs=[pl.BlockSpec((tm, d), row), pl.BlockSpec((1, d), const2),
                  _mod_spec(shift, blocks_per_group * (ROW_BLOCK // tm)),
                  _mod_spec(scale, blocks_per_group * (ROW_BLOCK // tm)),
                  pl.BlockSpec((qw, d), const2),
                  pl.BlockSpec((N_KEYS, N_KEYS), const2), pl.BlockSpec((N_KEYS, N_KEYS), const2)],
        out_specs=[pl.BlockSpec((d, tm), lambda i: (0, i)), fac_spec, fac_spec, fac_spec],
        out_shape=(jax.ShapeDtypeStruct((d, n), BF16), fac, fac, fac),
        scratch_shapes=[pltpu.VMEM((qw, tm), F32), key_scr, key_scr],
        compiler_params=_params("arbitrary"),
        name="peer_route",
    )(x, g, shift, scale, wqt, k1, k2)


def _gelu_twice(a):
    c = 0.7978845608028654
    return a * (1.0 + jnp.tanh(a * (c + (c * 0.044715) * (a * a))))


def _peer_gate_piece(a_scr, h_scr, e1_ref, e2_ref, th_ref, chunk, iis, jrows, cols, key_rows):
    te = a_scr.shape[0]
    for ii in iis:
        if ii not in key_rows:
            i_glob = chunk * (te // N_KEYS) + ii
            key_rows[ii] = [(th_ref[hd, pl.ds(i_glob, 1), :], e1_ref[hd, pl.ds(i_glob, 1), :])
                            for hd in range(PEER_HEADS)]
    gates = [None] * len(iis)
    for hd in range(PEER_HEADS):
        e2 = e2_ref[hd, jrows, cols]
        for n, ii in enumerate(iis):
            th, e1 = key_rows[ii][hd]
            term = jnp.where(e2 >= th[:, cols], e2, 0.0) * e1[:, cols]
            gates[n] = term if gates[n] is None else gates[n] + term
    for n, ii in enumerate(iis):
        rows = slice(ii * N_KEYS + jrows.start, ii * N_KEYS + jrows.stop)
        h_scr[rows, cols] = (_gelu_twice(a_scr[rows, cols]) * gates[n]).astype(BF16)


def _peer_kernel(xt_ref, u_ref, vt_ref, e1_ref, e2_ref, th_ref, yt_ref, a_scr, h_scr):
    c = pl.program_id(1)
    te, tm = a_scr.shape

    @pl.when(c == 0)
    def _():
        yt_ref[...] = jnp.zeros(yt_ref.shape, F32)

    a_scr[...] = _dot(u_ref[...], xt_ref[...])
    key_rows = {}
    half = N_KEYS // 2
    for ii in range(0, te // N_KEYS, 2):
        for k in range(tm // LANES):
            for jrows in (slice(0, half), slice(half, N_KEYS)):
                _peer_gate_piece(a_scr, h_scr, e1_ref, e2_ref, th_ref, c, (ii, ii + 1), jrows,
                                 slice(k * LANES, (k + 1) * LANES), key_rows)
    yt_ref[...] += lax.dot_general(vt_ref[...], h_scr[...], (((0,), (0,)), ((), ())),
                                   preferred_element_type=F32)


def _peer(xt, u, vt, e1, e2, th):
    d, n = xt.shape
    ne = u.shape[0]
    tm, te = ROW_BLOCK, EXPERT_BLOCK
    fac_spec = pl.BlockSpec((PEER_HEADS, N_KEYS, tm), lambda i, c: (0, 0, i))
    return pl.pallas_call(
        _peer_kernel,
        grid=(n // tm, ne // te),
        in_specs=[pl.BlockSpec((d, tm), lambda i, c: (0, i)),
                  pl.BlockSpec((te, d), lambda i, c: (c, 0)),
                  pl.BlockSpec((te, d), lambda i, c: (c, 0)),
                  fac_spec, fac_spec, fac_spec],
        out_specs=pl.BlockSpec((d, tm), lambda i, c: (0, i)),
        out_shape=jax.ShapeDtypeStruct((d, n), F32),
        scratch_shapes=[pltpu.VMEM((te, tm), F32), pltpu.VMEM((te, tm), BF16)],
        compiler_params=_params("arbitrary", "arbitrary"),
        name="peer_dense",
    )(xt, u, vt, e1, e2, th)


def _resid_kernel(x_ref, yt_ref, gate_ref, o_ref):
    o_ref[...] = x_ref[...] + gate_ref[0] * yt_ref[...].T


def _resid(x, yt, gate, *, blocks_per_group):
    n, d = x.shape
    tm = ROW_BLOCK
    row = lambda i: (i, 0)
    return pl.pallas_call(
        _resid_kernel,
        grid=(n // tm,),
        in_specs=[pl.BlockSpec((tm, d), row), pl.BlockSpec((d, tm), lambda i: (0, i)),
                  _mod_spec(gate, blocks_per_group)],
        out_specs=pl.BlockSpec((tm, d), row),
        out_shape=jax.ShapeDtypeStruct((n, d), F32),
        compiler_params=_params("arbitrary"),
        name="peer_resid",
    )(x, yt, gate)


class _Stream:
    def __init__(self, x, seq):
        self.batch = x.shape[0]
        self.seq = seq
        self.x = x.reshape(-1, x.shape[-1])
        self.long = seq % ROW_BLOCK == 0
        self.blocks_per_group = seq // ROW_BLOCK if self.long else 1

    def mod(self, m, k, tm=ROW_BLOCK):
        mk = m[:, k, :]
        if self.long:
            return mk[:, None, :]
        d = mk.shape[-1]
        return jnp.repeat(mk, self.seq, axis=0).reshape(-1, tm, d)


def kernel(x_prompt, x_sample, c_prompt, c_sample, cache_k, cache_v, cache_logf, page_table, state_conv,
           w_ada, b_ada, norm_mix, norm_ffn, w_in_ag, b_fgate, q_norm, k_norm, w_spatial, b_spatial,
           w_out_ag, w_in_conv, conv_w, conv_b, w_out_conv, peer_wq, peer_keys, peer_u, peer_v):
    bp, tp, d = x_prompt.shape
    bs, ts, _ = x_sample.shape
    depth = w_ada.shape[0]
    att_w = d // 2
    n_heads = att_w // HEAD_DIM
    n_groups = att_w // LANES
    assert tp % ROW_BLOCK == 0 and (bs * ts) % ROW_BLOCK == 0 and ROW_BLOCK % ts == 0 and ts == SUBLANES

    streams = [_Stream(x_prompt, tp), _Stream(x_sample, ts)]
    mod_all = _ada(jnp.concatenate([c_prompt, c_sample], axis=0), w_ada, b_ada)
    ck, cv = cache_k, cache_v
    clf = cache_logf.reshape(cache_logf.shape[0], cache_logf.shape[1], 1, PAGE_SIZE * n_heads)

    outs = {"k": [[], []], "v": [[], []], "f": [[], []], "g": [], "c": [[], []]}
    for layer in range(depth):
        ml = mod_all[layer].reshape(bp + bs, N_MOD, d)
        mods = [ml[:bp], ml[bp:]]
        g_mix = norm_mix[layer][None, :]
        g_ffn = norm_ffn[layer][None, :]
        if layer % 2 == 0:
            a = layer // 2
            w_in = w_in_ag[a]
            w5 = jnp.concatenate([w_in[:, :3 * att_w], w_in[:, 3 * att_w + n_heads:]], axis=1).astype(BF16)
            wf = jnp.pad(w_in[:, 3 * att_w:3 * att_w + n_heads], ((0, 0), (0, LANES - n_heads))).astype(BF16)
            bf = jnp.pad(b_fgate[a], (0, LANES - n_heads))[None, :]
            wa = w_out_ag[a][:att_w].astype(BF16)
            wg = w_out_ag[a][att_w:].astype(BF16)
            tril = jnp.tril(w_spatial[a])
        else:
            ci = layer // 2
            nj = d // CONV_COL_BLOCK
            w3 = w_in_conv[ci].reshape(d, 3, nj, CONV_COL_BLOCK).transpose(0, 2, 1, 3).reshape(d, 3 * d).astype(BF16)
            wo = w_out_conv[ci].astype(BF16)
        wqt = peer_wq[layer].T.astype(BF16)
        k1 = peer_keys[layer, 0].astype(BF16)
        k2 = peer_keys[layer, 1].astype(BF16)
        pu = peer_u[layer].astype(BF16)
        pvt = peer_v[layer].astype(BF16)

        for si, st in enumerate(streams):
            m = mods[si]
            bpg = st.blocks_per_group
            if layer % 2 == 0:
                q, k, v, u, gv, lf = _ag_in(st.x, g_mix, st.mod(m, 0), st.mod(m, 1), w5, wf, bf,
                                            q_norm[a][None, :], k_norm[a][None, :],
                                            blocks_per_group=bpg, q_dtype=BF16 if st.long else F32)
                if st.long:
                    f, ft = _cumsum(lf.reshape(st.batch, st.seq, LANES))
                    tk = min(ATTN_K_BLOCK, st.seq)
                    ft4 = ft[:, :n_heads, :].reshape(st.batch, n_heads, st.seq // tk, tk)
                    attn = _attn_prompt(q.reshape(st.batch, st.seq, att_w), k.reshape(st.batch, st.seq, att_w),
                                        v.reshape(st.batch, st.seq, att_w), f, ft4)
                    mix = tril.astype(BF16)
                    bias = jnp.tile(jnp.repeat(b_spatial[a].T, LANES, axis=1), (ROW_BLOCK // CHUNK, 1))
                else:
                    nr = st.seq * n_heads
                    lfn = jnp.pad(lf[:, :n_heads].reshape(st.batch, 1, nr), ((0, 0), (0, 0), (0, LANES - nr)))
                    attn = _attn_sample(page_table, q.reshape(st.batch, nr, HEAD_DIM),
                                        k.reshape(st.batch, nr, HEAD_DIM), v.reshape(st.batch, nr, HEAD_DIM),
                                        lfn, ck, cv, clf, a)
                    eye = jnp.eye(ROW_BLOCK // st.seq, dtype=F32)
                    mix = jax.vmap(lambda t: jnp.kron(eye, t))(tril[:, :st.seq, :st.seq]).astype(BF16)
                    bias = jnp.tile(jnp.repeat(b_spatial[a].T[:st.seq], LANES, axis=1), (ROW_BLOCK // st.seq, 1))
                    outs["g"].append(gv.reshape(st.batch, st.seq, att_w))
                st.x = _ag_out(attn.reshape(-1, att_w), u, gv, mix, bias, wa, wg, st.x, st.mod(m, 2),
                               blocks_per_group=bpg)
                outs["k"][si].append(k.reshape(st.batch, st.seq, n_heads, HEAD_DIM))
                outs["v"][si].append(v.reshape(st.batch, st.seq, n_heads, HEAD_DIM))
                outs["f"][si].append(lf[:, :n_heads].reshape(st.batch, st.seq, n_heads))
            else:
                bg, p = _conv_in(st.x, g_mix, st.mod(m, 0), st.mod(m, 1), w3, blocks_per_group=bpg)
                p3 = p.reshape(st.batch, st.seq, d)
                tc = CONV_ROW_BLOCK
                if st.long:
                    cb = st.seq // tc
                    pb = p.reshape(st.batch, cb, tc, d)[:, :, tc - SUBLANES:, :]
                    prev = jnp.concatenate([jnp.zeros_like(pb[:, :1]), pb[:, :-1]], axis=1)
                    prev = prev.reshape(st.batch * cb, SUBLANES, d)
                    s2 = prev[:, SUBLANES - 2:, :]
                    s2 = jnp.pad(s2, ((0, 0), (0, SUBLANES - 2), (0, 0)))
                    s1 = jnp.pad(prev[:, SUBLANES - 1:, :], ((0, 0), (0, SUBLANES - 1), (0, 0)))
                    period = tc
                else:
                    cb = 1
                    buf = state_conv[ci]
                    s2 = jnp.pad(buf, ((0, 0), (0, st.seq - 2), (0, 0))).reshape(-1, d)
                    s1 = jnp.pad(buf[:, 1:], ((0, 0), (0, st.seq - 1), (0, 0))).reshape(-1, d)
                    period = st.seq
                st.x = _conv_out(p, s1, s2, bg, conv_w[ci], conv_b[ci][None, :], wo, st.x, st.mod(m, 2, tc),
                                 blocks_per_group=cb, period=period)
                outs["c"][si].append(p3[:, st.seq - 2:, :])
            xt, e1, e2, th = _route(st.x, g_ffn, st.mod(m, 3, ROUTE_BLOCK), st.mod(m, 4, ROUTE_BLOCK),
                                    wqt, k1, k2, blocks_per_group=bpg)
            yt = _peer(xt, pu, pvt, e1, e2, th)
            st.x = _resid(st.x, yt, st.mod(m, 5), blocks_per_group=bpg)

    sp, ss = streams
    return (sp.x.reshape(bp, tp, d), ss.x.reshape(bs, ts, d),
            jnp.stack(outs["k"][0]), jnp.stack(outs["v"][0]), jnp.stack(outs["f"][0]),
            jnp.stack(outs["k"][1]), jnp.stack(outs["v"][1]), jnp.stack(outs["f"][1]),
            jnp.stack(outs["g"]), jnp.stack(outs["c"][0]), jnp.stack(outs["c"][1]))
```
